```python
import jax
import jax.numpy as jnp
from jax import lax
import numpy as np

D_MODEL = 2048
BATCH = 16
SEQ = 2048
DEPTH = 1
DEC_BATCH = 8
DEC_SEQ = 32
PAST_LEN = 1024

CHUNK = 64
D_CONV = 1024
CONV_WIDTH = 31
N_HEADS = 16
N_KV_HEADS = 4
HEAD_DIM = 64
GROUP = N_HEADS // N_KV_HEADS
D_ATTN = N_HEADS * HEAD_DIM
KV_DIM = N_KV_HEADS * HEAD_DIM
D_MIX = D_CONV + D_ATTN
IN_COLS = 2 * D_CONV + D_ATTN + 2 * KV_DIM
ROT_DIM = HEAD_DIM // 4
ROPE_THETA = 500000.0
WINDOW = 128
WINDOW_CHUNKS = WINDOW // CHUNK
N_MEM = 256
MEM_HEADS = 4
MEM_HEAD_DIM = 128
D_MEM = MEM_HEADS * MEM_HEAD_DIM
PEER_HEADS = 8
N_KEYS = 128
N_EXPERTS = N_KEYS * N_KEYS
PEER_DK = 256
PEER_TOPK = 16
PEER_BLOCK = 256
EPS = 1e-6
NEG = -1e30

kernel_name = "hymba_conformer_swa_peer_stream_step"


def rmsnorm(x, g):
    xf = x.astype(jnp.float32)
    y = xf * lax.rsqrt(jnp.mean(xf * xf, axis=-1, keepdims=True) + EPS)
    return (y * g.astype(jnp.float32)).astype(x.dtype)


def layernorm(x, g, b):
    xf = x.astype(jnp.float32)
    mu = jnp.mean(xf, axis=-1, keepdims=True)
    var = jnp.mean(jnp.square(xf - mu), axis=-1, keepdims=True)
    y = (xf - mu) * lax.rsqrt(var + EPS) * g.astype(jnp.float32) + b.astype(jnp.float32)
    return y.astype(x.dtype)


def partial_rope(x, pos):
    half = ROT_DIM // 2
    inv_freq = jnp.power(jnp.float32(ROPE_THETA), -(2.0 * jnp.arange(half, dtype=jnp.float32)) / ROT_DIM)
    ang = pos.astype(jnp.float32)[:, None] * inv_freq[None, :]
    cos = jnp.cos(ang)[None, :, None, :]
    sin = jnp.sin(ang)[None, :, None, :]
    xr = x[..., :ROT_DIM].astype(jnp.float32)
    x1, x2 = xr[..., :half], xr[..., half:]
    rot = jnp.concatenate([x1 * cos - x2 * sin, x2 * cos + x1 * sin], axis=-1).astype(x.dtype)
    return jnp.concatenate([rot, x[..., ROT_DIM:]], axis=-1)


def sink_gqa_attention(q, k, v, sinks, mask):
    lead = q.shape[:-3]
    tq, hd = q.shape[-3], q.shape[-1]
    qg = q.reshape(lead + (tq, N_KV_HEADS, GROUP, hd))
    s = jnp.einsum('...qkgd,...skd->...kgqs', qg, k).astype(jnp.float32) * (hd ** -0.5)
    if mask is not None:
        s = jnp.where(mask, s, NEG)
    sink = jnp.broadcast_to(sinks.astype(jnp.float32).reshape(N_KV_HEADS, GROUP, 1, 1), s.shape[:-1] + (1,))
    p = jax.nn.softmax(jnp.concatenate([s, sink], axis=-1), axis=-1)[..., :-1]
    o = jnp.einsum('...kgqs,...skd->...qkgd', p.astype(v.dtype), v)
    return o.reshape(lead + (tq, N_HEADS, hd))


def swa_prompt(q, k, v, sinks):
    b, t = q.shape[:2]
    nc = t // CHUNK
    qc = q.reshape(b, nc, CHUNK, N_HEADS, HEAD_DIM)

    def band(a):
        ac = a.reshape(b, nc, CHUNK, N_KV_HEADS, HEAD_DIM)
        ap = jnp.pad(ac, ((0, 0), (WINDOW_CHUNKS, 0), (0, 0), (0, 0), (0, 0)))
        return jnp.concatenate([ap[:, j:j + nc] for j in range(WINDOW_CHUNKS + 1)], axis=2)

    kb, vb = band(k), band(v)
    offs = jnp.repeat(jnp.arange(WINDOW_CHUNKS + 1) - WINDOW_CHUNKS, CHUNK)
    key_chunk = jnp.arange(nc)[:, None] + offs[None, :]
    mask = (key_chunk >= 0)[None, :, None, None, None, :]
    o = sink_gqa_attention(qc, kb, vb, sinks, mask)
    return o.reshape(b, t, D_ATTN)


def conformer_conv(a, hist, dw_w, dw_b, ln_g, ln_b):
    glu = a[..., :D_CONV] * jax.nn.sigmoid(a[..., D_CONV:])
    xin = jnp.concatenate([hist.astype(glu.dtype), glu], axis=1)
    y = lax.conv_general_dilated(xin, dw_w[:, None, :].astype(xin.dtype), window_strides=(1,), padding='VALID',
                                 dimension_numbers=('NWC', 'WIO', 'NWC'), feature_group_count=D_CONV)
    y = jax.nn.silu(layernorm(y + dw_b, ln_g, ln_b))
    return y, xin[:, -(CONV_WIDTH - 1):]


def memory_kv(mem, norm_mem_g, w_ck, w_cv):
    b = mem.shape[0]
    mn = rmsnorm(mem, norm_mem_g)
    mk = (mn @ w_ck).reshape(b, -1, MEM_HEADS, MEM_HEAD_DIM)
    mv = (mn @ w_cv).reshape(b, -1, MEM_HEADS, MEM_HEAD_DIM)
    return mk, mv


def cross_attention(xn, mk, mv, w_cq, w_co):
    b, t = xn.shape[:2]
    q = (xn @ w_cq).reshape(b, t, MEM_HEADS, MEM_HEAD_DIM)
    s = jnp.einsum('bqhd,bshd->bhqs', q, mk).astype(jnp.float32) * (MEM_HEAD_DIM ** -0.5)
    p = jax.nn.softmax(s, axis=-1)
    o = jnp.einsum('bhqs,bshd->bqhd', p.astype(mv.dtype), mv).reshape(b, t, D_MEM)
    return o @ w_co


def peer_ffn(xn, w_q, sub_keys, u, v):
    b, t, d = xn.shape
    n = b * t
    xf = xn.reshape(n, d)
    q = (xf @ w_q).reshape(n, PEER_HEADS, 2, PEER_DK // 2)
    s = jnp.einsum('nhpd,hpkd->nhpk', q, sub_keys).astype(jnp.float32)
    sv, si = lax.top_k(s, PEER_TOPK)
    cand = (sv[:, :, 0, :, None] + sv[:, :, 1, None, :]).reshape(n, PEER_HEADS, PEER_TOPK * PEER_TOPK)
    cidx = (si[:, :, 0, :, None] * N_KEYS + si[:, :, 1, None, :]).reshape(n, PEER_HEADS, PEER_TOPK * PEER_TOPK)
    tv, tsel = lax.top_k(cand, PEER_TOPK)
    eidx = jnp.take_along_axis(cidx, tsel, axis=-1).reshape(n, PEER_HEADS * PEER_TOPK)
    g = jax.nn.softmax(tv, axis=-1).reshape(n, PEER_HEADS * PEER_TOPK).astype(xn.dtype)
    nb = -(-n // PEER_BLOCK)
    pad = nb * PEER_BLOCK - n
    xb = jnp.pad(xf, ((0, pad), (0, 0))).reshape(nb, PEER_BLOCK, d)
    ib = jnp.pad(eidx, ((0, pad), (0, 0))).reshape(nb, PEER_BLOCK, PEER_HEADS * PEER_TOPK)
    gb = jnp.pad(g, ((0, pad), (0, 0))).reshape(nb, PEER_BLOCK, PEER_HEADS * PEER_TOPK)

    def expert_block(args):
        xt, it, gt = args
        act = jax.nn.gelu(jnp.einsum('nd,ned->ne', xt, u[it]), approximate=False) * gt
        return jnp.einsum('ne,ned->nd', act, v[it])

    y = lax.map(expert_block, (xb, ib, gb)).reshape(nb * PEER_BLOCK, d)[:n]
    return y.reshape(b, t, d)


def layer_forward(h, pos, conv_hist, swa_k_hist, swa_v_hist, mem_k, mem_v,
                  norm_mix_g, w_in, conv_dw_w, conv_dw_b, conv_ln_g, conv_ln_b, swa_sinks, w_out,
                  norm_cross_g, w_cq, w_co, norm_ffn_g, peer_wq, peer_subkeys, peer_u, peer_v):
    b, t = h.shape[:2]
    xn = rmsnorm(h, norm_mix_g)
    proj = xn @ w_in
    c0 = 2 * D_CONV
    a = proj[..., :c0]
    q = partial_rope(proj[..., c0:c0 + D_ATTN].reshape(b, t, N_HEADS, HEAD_DIM), pos)
    k = partial_rope(proj[..., c0 + D_ATTN:c0 + D_ATTN + KV_DIM].reshape(b, t, N_KV_HEADS, HEAD_DIM), pos)
    v = proj[..., c0 + D_ATTN + KV_DIM:].reshape(b, t, N_KV_HEADS, HEAD_DIM)
    if conv_hist is None:
        conv_hist = jnp.zeros((b, CONV_WIDTH - 1, D_CONV), dtype=a.dtype)
    conv_out, new_conv = conformer_conv(a, conv_hist, conv_dw_w, conv_dw_b, conv_ln_g, conv_ln_b)
    if swa_k_hist is None:
        attn_out = swa_prompt(q, k, v, swa_sinks)
        new_k, new_v = k[:, -WINDOW:], v[:, -WINDOW:]
    else:
        k_all = jnp.concatenate([swa_k_hist.astype(k.dtype), k], axis=1)
        v_all = jnp.concatenate([swa_v_hist.astype(v.dtype), v], axis=1)
        attn_out = sink_gqa_attention(q, k_all, v_all, swa_sinks, None).reshape(b, t, D_ATTN)
        new_k, new_v = k_all[:, -WINDOW:], v_all[:, -WINDOW:]
    h = h + jnp.concatenate([conv_out, attn_out], axis=-1) @ w_out
    h = h + cross_attention(rmsnorm(h, norm_cross_g), mem_k, mem_v, w_cq, w_co)
    h = h + peer_ffn(rmsnorm(h, norm_ffn_g), peer_wq, peer_subkeys, peer_u, peer_v)
    return h, new_conv, new_k, new_v


def setup_inputs(seed: int = 0) -> dict:
    key = jax.random.key(seed)
    ks = jax.random.split(key, 32)

    def nrm(k, shape, scale):
        return jax.random.normal(k, shape, jnp.float32) * scale

    def gain(k, shape):
        return 1.0 + 0.01 * jax.random.normal(k, shape, jnp.float32)

    return {
        "x_prompt": nrm(ks[0], (BATCH, SEQ, D_MODEL), 1.0),
        "x_sample": nrm(ks[1], (DEC_BATCH, DEC_SEQ, D_MODEL), 1.0),
        "mem_prompt": nrm(ks[2], (BATCH, N_MEM, D_MODEL), 1.0),
        "cache_conv": nrm(ks[3], (DEPTH, DEC_BATCH, CONV_WIDTH - 1, D_CONV), 0.5),
        "cache_swa_k": nrm(ks[4], (DEPTH, DEC_BATCH, WINDOW, N_KV_HEADS, HEAD_DIM), 1.0),
        "cache_swa_v": nrm(ks[5], (DEPTH, DEC_BATCH, WINDOW, N_KV_HEADS, HEAD_DIM), 1.0),
        "cache_mem_k": nrm(ks[6], (DEPTH, DEC_BATCH, N_MEM, MEM_HEADS, MEM_HEAD_DIM), 1.0),
        "cache_mem_v": nrm(ks[7], (DEPTH, DEC_BATCH, N_MEM, MEM_HEADS, MEM_HEAD_DIM), 1.0),
        "norm_mix_g": gain(ks[8], (DEPTH, D_MODEL)),
        "w_in": nrm(ks[9], (DEPTH, D_MODEL, IN_COLS), D_MODEL ** -0.5),
        "conv_dw_w": nrm(ks[10], (DEPTH, CONV_WIDTH, D_CONV), CONV_WIDTH ** -0.5),
        "conv_dw_b": nrm(ks[11], (DEPTH, D_CONV), 0.01),
        "conv_ln_g": gain(ks[12], (DEPTH, D_CONV)),
        "conv_ln_b": nrm(ks[13], (DEPTH, D_CONV), 0.01),
        "swa_sinks": nrm(ks[14], (DEPTH, N_HEADS), 0.5),
        "w_out": nrm(ks[15], (DEPTH, D_MIX, D_MODEL), D_MIX ** -0.5),
        "norm_cross_g": gain(ks[16], (DEPTH, D_MODEL)),
        "norm_mem_g": gain(ks[17], (DEPTH, D_MODEL)),
        "w_cq": nrm(ks[18], (DEPTH, D_MODEL, D_MEM), D_MODEL ** -0.5),
        "w_ck": nrm(ks[19], (DEPTH, D_MODEL, D_MEM), D_MODEL ** -0.5),
        "w_cv": nrm(ks[20], (DEPTH, D_MODEL, D_MEM), D_MODEL ** -0.5),
        "w_co": nrm(ks[21], (DEPTH, D_MEM, D_MODEL), D_MEM ** -0.5),
        "norm_ffn_g": gain(ks[22], (DEPTH, D_MODEL)),
        "peer_wq": nrm(ks[23], (DEPTH, D_MODEL, PEER_HEADS * PEER_DK), D_MODEL ** -0.5),
        "peer_subkeys": nrm(ks[24], (DEPTH, PEER_HEADS, 2, N_KEYS, PEER_DK // 2), (PEER_DK // 2) ** -0.5),
        "peer_u": nrm(ks[25], (DEPTH, N_EXPERTS, D_MODEL), D_MODEL ** -0.5),
        "peer_v": nrm(ks[26], (DEPTH, N_EXPERTS, D_MODEL), PEER_HEADS ** -0.5),
        "norm_final_g": gain(ks[27], (D_MODEL,)),
    }


def reference(x_prompt, x_sample, mem_prompt, cache_conv, cache_swa_k, cache_swa_v, cache_mem_k, cache_mem_v,
              norm_mix_g, w_in, conv_dw_w, conv_dw_b, conv_ln_g, conv_ln_b, swa_sinks, w_out,
              norm_cross_g, norm_mem_g, w_cq, w_ck, w_cv, w_co,
              norm_ffn_g, peer_wq, peer_subkeys, peer_u, peer_v, norm_final_g):
    pos_p = jnp.arange(x_prompt.shape[1], dtype=jnp.int32)
    pos_s = PAST_LEN + jnp.arange(x_sample.shape[1], dtype=jnp.int32)
    hp, hs = x_prompt, x_sample
    conv_p, k_p, v_p, mk_p, mv_p = [], [], [], [], []
    conv_s, k_s, v_s = [], [], []
    for l in range(DEPTH):
        lw = (norm_mix_g[l], w_in[l], conv_dw_w[l], conv_dw_b[l], conv_ln_g[l], conv_ln_b[l], swa_sinks[l], w_out[l],
              norm_cross_g[l], w_cq[l], w_co[l], norm_ffn_g[l], peer_wq[l], peer_subkeys[l], peer_u[l], peer_v[l])
        mk, mv = memory_kv(mem_prompt, norm_mem_g[l], w_ck[l], w_cv[l])
        hp, c_new, k_new, v_new = layer_forward(hp, pos_p, None, None, None, mk, mv, *lw)
        conv_p.append(c_new)
        k_p.append(k_new)
        v_p.append(v_new)
        mk_p.append(mk)
        mv_p.append(mv)
        hs, c_new, k_new, v_new = layer_forward(hs, pos_s, cache_conv[l], cache_swa_k[l], cache_swa_v[l],
                                                cache_mem_k[l], cache_mem_v[l], *lw)
        conv_s.append(c_new)
        k_s.append(k_new)
        v_s.append(v_new)
    y_prompt = rmsnorm(hp, norm_final_g)
    y_sample = rmsnorm(hs, norm_final_g)
    return (y_prompt, y_sample, jnp.stack(conv_p), jnp.stack(k_p), jnp.stack(v_p), jnp.stack(mk_p), jnp.stack(mv_p),
            jnp.stack(conv_s), jnp.stack(k_s), jnp.stack(v_s))
```

```python
import functools

import jax
import jax.numpy as jnp
from jax import lax
from jax.experimental import pallas as pl
from jax.experimental.pallas import tpu as pltpu

F32 = jnp.float32
BF16 = jnp.bfloat16

D_MODEL = 2048
PAST_LEN = 1024
CHUNK = 64
D_CONV = 1024
CONV_WIDTH = 31
HIST = CONV_WIDTH - 1
HIST_PAD = 32
N_HEADS = 16
N_KV_HEADS = 4
HEAD_DIM = 64
GROUP = N_HEADS // N_KV_HEADS
D_ATTN = N_HEADS * HEAD_DIM
KV_DIM = N_KV_HEADS * HEAD_DIM
IN_COLS = 2 * D_CONV + D_ATTN + 2 * KV_DIM
ROT_DIM = HEAD_DIM // 4
ROT_HALF = ROT_DIM // 2
ROPE_THETA = 500000.0
WINDOW = 128
WINDOW_CHUNKS = WINDOW // CHUNK
N_MEM = 256
MEM_HEADS = 4
MEM_HEAD_DIM = 128
D_MEM = MEM_HEADS * MEM_HEAD_DIM
PEER_HEADS = 8
N_KEYS = 128
N_EXPERTS = N_KEYS * N_KEYS
PEER_DK = 256
PEER_TOPK = 16
EPS = 1e-6
NEG = -1e30

LANES = 128
VMEM_LIMIT = 56 * 1024 * 1024


def _params(*sem):
    return pltpu.CompilerParams(dimension_semantics=sem, vmem_limit_bytes=VMEM_LIMIT)


def _resident(shape):
    return pl.BlockSpec(shape, lambda *_: (0,) * len(shape), pipeline_mode=pl.Buffered(1))


def _rms(x, g):
    return x * lax.rsqrt(jnp.mean(x * x, axis=-1, keepdims=True) + EPS) * g


def _dot(a, b):
    return jnp.dot(a, b, preferred_element_type=F32)


def _dot_nt(a, b):
    return lax.dot_general(a, b, (((1,), (1,)), ((), ())), preferred_element_type=F32)


def _dot_tn(a, b):
    return lax.dot_general(a, b, (((0,), (0,)), ((), ())), preferred_element_type=F32)


def _rope(x, cos, sin_up, sin_dn):
    return (x * cos + pltpu.roll(x, ROT_HALF, 1) * sin_up
            + pltpu.roll(x, LANES - ROT_HALF, 1) * sin_dn)


def _in_proj_kernel(x_ref, g_ref, w_ref, cos_ref, sup_ref, sdn_ref, glu_ref, q_ref, k_ref, v_ref):
    xn = _rms(x_ref[...], g_ref[...]).astype(BF16)
    a1 = _dot(xn, w_ref[:, 0:D_CONV])
    a2 = _dot(xn, w_ref[:, D_CONV:2 * D_CONV])
    glu_ref[...] = a1 / (1.0 + jnp.exp(-a2))
    cos, sup, sdn = cos_ref[...], sup_ref[...], sdn_ref[...]
    c0 = 2 * D_CONV
    for c in range(D_ATTN // 256):
        qc = _dot(xn, w_ref[:, c0 + 256 * c:c0 + 256 * (c + 1)])
        q_ref[:, 256 * c:256 * c + 128] = _rope(qc[:, 0:128], cos, sup, sdn)
        q_ref[:, 256 * c + 128:256 * (c + 1)] = _rope(qc[:, 128:256], cos, sup, sdn)
    c1 = c0 + D_ATTN
    kc = _dot(xn, w_ref[:, c1:c1 + KV_DIM])
    k_ref[:, 0:128] = _rope(kc[:, 0:128], cos, sup, sdn)
    k_ref[:, 128:256] = _rope(kc[:, 128:256], cos, sup, sdn)
    v_ref[...] = _dot(xn, w_ref[:, c1 + KV_DIM:c1 + 2 * KV_DIM])


def _rope_tables(t, offset):
    pos = (offset + jnp.arange(t, dtype=jnp.int32)).astype(F32)
    inv_freq = jnp.power(jnp.float32(ROPE_THETA), -(2.0 * jnp.arange(ROT_HALF, dtype=F32)) / ROT_DIM)
    ang = pos[:, None] * inv_freq[None, :]
    d = jnp.arange(LANES) % HEAD_DIM
    ang_l = ang[:, d % ROT_HALF]
    cos = jnp.where(d[None, :] < ROT_DIM, jnp.cos(ang_l), 1.0)
    sin = jnp.sin(ang_l)
    sin_up = jnp.where((d[None, :] >= ROT_HALF) & (d[None, :] < ROT_DIM), sin, 0.0)
    sin_dn = jnp.where(d[None, :] < ROT_HALF, -sin, 0.0)
    return cos.astype(F32), sin_up.astype(F32), sin_dn.astype(F32)


def _in_proj(x, g, w_bf, pos_offset, tb):
    b, t, d = x.shape
    cos, sup, sdn = _rope_tables(t, pos_offset)
    row = lambda c: pl.BlockSpec((None, tb, c), lambda i, j: (i, j, 0))
    tab = pl.BlockSpec((tb, LANES), lambda i, j: (j, 0))
    return pl.pallas_call(
        _in_proj_kernel,
        grid=(b, t // tb),
        in_specs=[row(d), _resident((1, d)), _resident((d, IN_COLS)), tab, tab, tab],
        out_specs=[row(D_CONV), row(D_ATTN), row(KV_DIM), row(KV_DIM)],
        out_shape=[jax.ShapeDtypeStruct((b, t, c), F32) for c in (D_CONV, D_ATTN, KV_DIM, KV_DIM)],
        compiler_params=_params("parallel", "arbitrary"),
    )(x, g.reshape(1, d), w_bf, cos, sup, sdn)


def _conv_kernel(glu_ref, hist_ref, w_ref, b_ref, lg_ref, lb_ref, o_ref, xin_ref, y_ref, *, tb):
    @pl.when(pl.program_id(1) == 0)
    def _():
        xin_ref[0:HIST_PAD, :] = hist_ref[...]

    xin_ref[HIST_PAD:HIST_PAD + tb, :] = glu_ref[...]
    lead = HIST_PAD - HIST
    for c in range(D_CONV // LANES):
        cs = slice(c * LANES, (c + 1) * LANES)
        acc = jnp.zeros((tb, LANES), F32)
        for k in range(CONV_WIDTH):
            acc = acc + xin_ref[lead + k:lead + k + tb, cs] * w_ref[k:k + 1, cs]
        y_ref[:, cs] = acc + b_ref[:, cs]
    y = y_ref[...]
    mu = jnp.mean(y, axis=-1, keepdims=True)
    yc = y - mu
    var = jnp.mean(yc * yc, axis=-1, keepdims=True)
    z = yc * lax.rsqrt(var + EPS) * lg_ref[...] + lb_ref[...]
    o_ref[...] = z / (1.0 + jnp.exp(-z))
    xin_ref[0:HIST_PAD, :] = xin_ref[tb:tb + HIST_PAD, :]


def _conv(glu, hist, w, bias, ln_g, ln_b, tb):
    b, t, c = glu.shape
    hist_p = jnp.pad(hist.astype(F32), ((0, 0), (HIST_PAD - HIST, 0), (0, 0)))
    w_p = jnp.pad(w, ((0, HIST_PAD - CONV_WIDTH), (0, 0)))
    row = pl.BlockSpec((None, tb, c), lambda i, j: (i, j, 0))
    return pl.pallas_call(
        functools.partial(_conv_kernel, tb=tb),
        grid=(b, t // tb),
        in_specs=[row, pl.BlockSpec((None, HIST_PAD, c), lambda i, j: (i, 0, 0)),
                  _resident((HIST_PAD, c)), _resident((1, c)), _resident((1, c)), _resident((1, c))],
        out_specs=row,
        out_shape=jax.ShapeDtypeStruct((b, t, c), F32),
        scratch_shapes=[pltpu.VMEM((tb + HIST_PAD, c), F32), pltpu.VMEM((tb, c), F32)],
        compiler_params=_params("parallel", "arbitrary"),
    )(glu, hist_p, w_p, bias.reshape(1, c), ln_g.reshape(1, c), ln_b.reshape(1, c))


def _attn_kernel(sinks_ref, q_ref, *refs, n_kb, banded):
    k_refs, v_refs, o_ref = refs[:n_kb], refs[n_kb:2 * n_kb], refs[2 * n_kb]
    q = q_ref[...]
    tq = q.shape[0]
    kk = jnp.concatenate([r[...] for r in k_refs], axis=0).astype(BF16)
    vv = jnp.concatenate([r[...] for r in v_refs], axis=0).astype(BF16)
    tk = kk.shape[0]
    if banded:
        blk = lax.broadcasted_iota(jnp.int32, (1, tk), 1) // k_refs[0].shape[0]
        valid = (pl.program_id(1) + blk) >= (n_kb - 1)
    outs = []
    for g in range(N_KV_HEADS):
        heads = range(g * GROUP, (g + 1) * GROUP)
        qg = jnp.concatenate([q[:, h * HEAD_DIM:(h + 1) * HEAD_DIM] for h in heads], axis=0)
        kg = kk[:, g * HEAD_DIM:(g + 1) * HEAD_DIM]
        vg = vv[:, g * HEAD_DIM:(g + 1) * HEAD_DIM]
        s = _dot_nt(qg.astype(BF16), kg) * (HEAD_DIM ** -0.5)
        if banded:
            s = jnp.where(valid, s, NEG)
        sink = jnp.concatenate([jnp.full((tq, 1), sinks_ref[h], F32) for h in heads], axis=0)
        m = jnp.maximum(jnp.max(s, axis=-1, keepdims=True), sink)
        p = jnp.exp(s - m)
        den = jnp.sum(p, axis=-1, keepdims=True) + jnp.exp(sink - m)
        og = _dot((p / den).astype(BF16), vg)
        outs += [og[i * tq:(i + 1) * tq, :] for i in range(GROUP)]
    o_ref[...] = jnp.concatenate(outs, axis=1)


def _attn_prompt(q, k, v, sinks):
    b, t, _ = q.shape
    nkb = WINDOW_CHUNKS + 1
    qspec = pl.BlockSpec((None, CHUNK, D_ATTN), lambda i, c: (i, c, 0))

    def kspec(back):
        return pl.BlockSpec((None, CHUNK, KV_DIM), lambda i, c: (i, jnp.maximum(c - back, 0), 0))

    kspecs = [kspec(nkb - 1 - j) for j in range(nkb)]
    return pl.pallas_call(
        functools.partial(_attn_kernel, n_kb=nkb, banded=True),
        grid=(b, t // CHUNK),
        in_specs=[pl.BlockSpec(memory_space=pltpu.SMEM), qspec] + kspecs + kspecs,
        out_specs=qspec,
        out_shape=jax.ShapeDtypeStruct((b, t, D_ATTN), F32),
        compiler_params=_params("parallel", "arbitrary"),
    )(sinks, q, *([k] * nkb), *([v] * nkb))


def _attn_cached(q, k_hist, v_hist, k, v, sinks):
    b, t, _ = q.shape
    th = k_hist.shape[1]
    qspec = pl.BlockSpec((None, t, D_ATTN), lambda i: (i, 0, 0))
    hspec = pl.BlockSpec((None, th, KV_DIM), lambda i: (i, 0, 0))
    nspec = pl.BlockSpec((None, t, KV_DIM), lambda i: (i, 0, 0))
    return pl.pallas_call(
        functools.partial(_attn_kernel, n_kb=2, banded=False),
        grid=(b,),
        in_specs=[pl.BlockSpec(memory_space=pltpu.SMEM), qspec, hspec, nspec, hspec, nspec],
        out_specs=qspec,
        out_shape=jax.ShapeDtypeStruct((b, t, D_ATTN), F32),
        compiler_params=_params("parallel"),
    )(sinks, q, k_hist, k, v_hist, v)


def _mem_kv_kernel(m_ref, g_ref, wk_ref, wv_ref, k_ref, v_ref):
    mn = _rms(m_ref[...], g_ref[...]).astype(BF16)
    k_ref[...] = _dot(mn, wk_ref[...])
    v_ref[...] = _dot(mn, wv_ref[...])


def _mem_kv(mem, g, wk_bf, wv_bf):
    b, n, d = mem.shape
    out = pl.BlockSpec((None, n, D_MEM), lambda i: (i, 0, 0))
    return pl.pallas_call(
        _mem_kv_kernel,
        grid=(b,),
        in_specs=[pl.BlockSpec((None, n, d), lambda i: (i, 0, 0)), _resident((1, d)),
                  _resident((d, D_MEM)), _resident((d, D_MEM))],
        out_specs=[out, out],
        out_shape=[jax.ShapeDtypeStruct((b, n, D_MEM), F32)] * 2,
        compiler_params=_params("parallel"),
    )(mem, g.reshape(1, d), wk_bf, wv_bf)


def _mid_kernel(x_ref, conv_ref, attn_ref, wout_ref, gc_ref, wcq_ref, mk_ref, mv_ref, wco_ref,
                gf_ref, wq_ref, h2_ref, xn_ref, qp_ref):
    h1 = (x_ref[...] + _dot(conv_ref[...].astype(BF16), wout_ref[0:D_CONV, :])
          + _dot(attn_ref[...].astype(BF16), wout_ref[D_CONV:D_CONV + D_ATTN, :]))
    qc = _dot(_rms(h1, gc_ref[...]).astype(BF16), wcq_ref[...])
    mk = mk_ref[...].astype(BF16)
    mv = mv_ref[...].astype(BF16)
    outs = []
    for h in range(MEM_HEADS):
        hs = slice(h * MEM_HEAD_DIM, (h + 1) * MEM_HEAD_DIM)
        s = _dot_nt(qc[:, hs].astype(BF16), mk[:, hs]) * (MEM_HEAD_DIM ** -0.5)
        p = jnp.exp(s - jnp.max(s, axis=-1, keepdims=True))
        p = p / jnp.sum(p, axis=-1, keepdims=True)
        outs.append(_dot(p.astype(BF16), mv[:, hs]))
    o = jnp.concatenate(outs, axis=1).astype(BF16)
    h2 = h1 + _dot(o, wco_ref[...])
    h2_ref[...] = h2
    xn = _rms(h2, gf_ref[...]).astype(BF16)
    xn_ref[...] = xn
    qp_ref[...] = _dot(xn, wq_ref[...])


def _mid(x, conv_out, attn_out, mk, mv, wout_bf, gc, wcq_bf, wco_bf, gf, wq_bf, tb):
    b, t, d = x.shape
    row = lambda c: pl.BlockSpec((None, tb, c), lambda i, j: (i, j, 0))
    mem = pl.BlockSpec((None, N_MEM, D_MEM), lambda i, j: (i, 0, 0))
    dq = wq_bf.shape[1]
    return pl.pallas_call(
        _mid_kernel,
        grid=(b, t // tb),
        in_specs=[row(d), row(D_CONV), row(D_ATTN), _resident((D_CONV + D_ATTN, d)), _resident((1, d)),
                  _resident((d, D_MEM)), mem, mem, _resident((D_MEM, d)), _resident((1, d)),
                  _resident((d, dq))],
        out_specs=[row(d), row(d), row(dq)],
        out_shape=[jax.ShapeDtypeStruct((b, t, d), F32), jax.ShapeDtypeStruct((b, t, d), BF16),
                   jax.ShapeDtypeStruct((b, t, dq), F32)],
        compiler_params=_params("parallel", "arbitrary"),
    )(x, conv_out, attn_out, wout_bf, gc.reshape(1, d), wcq_bf, mk, mv, wco_bf, gf.reshape(1, d), wq_bf)


def _topk_rank(work_ref, rank_ref, sv_ref):
    n, tr = work_ref.shape
    rows = lax.broadcasted_iota(jnp.int32, (n, tr), 0).astype(F32)
    rank_ref[...] = jnp.full((n, tr), float(PEER_TOPK), F32)

    def body(k, carry):
        w = work_ref[...]
        m = jnp.max(w, axis=0, keepdims=True)
        first = jnp.min(jnp.where(w == m, rows, float(n)), axis=0, keepdims=True)
        hit = rows == first
        sv_ref[pl.ds(k, 1), :] = m
        rank_ref[...] = jnp.where(hit, k.astype(F32), rank_ref[...])
        work_ref[...] = jnp.where(hit, -jnp.inf, w)
        return carry

    lax.fori_loop(0, PEER_TOPK, body, 0)


def _router_kernel(q_ref, keys_ref, lf_ref, e1_ref, r2_ref, e2_ref,
                   w1_ref, w2_ref, r1_ref, sv1_ref, sv2_ref, cand_ref, rc_ref, svc_ref):
    q = q_ref[...]
    half = PEER_DK // 2
    s1 = _dot_nt(keys_ref[0], q[:, 0:half])
    s2 = _dot_nt(keys_ref[1], q[:, half:PEER_DK])
    w1_ref[...] = s1
    w2_ref[...] = s2
    _topk_rank(w1_ref, r1_ref, sv1_ref)
    _topk_rank(w2_ref, r2_ref, sv2_ref)
    sv2 = sv2_ref[...]
    for k1 in range(PEER_TOPK):
        cand_ref[k1 * PEER_TOPK:(k1 + 1) * PEER_TOPK, :] = sv1_ref[k1:k1 + 1, :] + sv2
    _topk_rank(cand_ref, rc_ref, svc_ref)
    g2 = jnp.exp(sv2 - sv2_ref[0:1, :])
    r1 = r1_ref[...]
    lf = jnp.zeros_like(r1)
    z = jnp.zeros_like(sv2[0:1, :])
    for k1 in range(PEER_TOPK):
        sel = rc_ref[k1 * PEER_TOPK:(k1 + 1) * PEER_TOPK, :] < float(PEER_TOPK)
        cnt = jnp.sum(jnp.where(sel, 1.0, 0.0), axis=0, keepdims=True)
        zk = jnp.sum(jnp.where(sel, g2, 0.0), axis=0, keepdims=True)
        z = z + jnp.exp(sv1_ref[k1:k1 + 1, :] - sv1_ref[0:1, :]) * zk
        lf = lf + jnp.where(r1 == float(k1), cnt, 0.0)
    lf_ref[...] = lf
    e1_ref[...] = jnp.exp(s1 - sv1_ref[0:1, :]) / z
    e2_ref[...] = jnp.exp(s2 - sv2_ref[0:1, :])


def _router(qp, keys, tr):
    n = qp.shape[0]
    tab = pl.BlockSpec((None, N_KEYS, tr), lambda t, h: (h, 0, t))
    ncand = PEER_TOPK * PEER_TOPK
    return pl.pallas_call(
        _router_kernel,
        grid=(n // tr, PEER_HEADS),
        in_specs=[pl.BlockSpec((tr, PEER_DK), lambda t, h: (t, h)),
                  pl.BlockSpec((None, 2, N_KEYS, PEER_DK // 2), lambda t, h: (h, 0, 0, 0))],
        out_specs=[tab] * 4,
        out_shape=[jax.ShapeDtypeStruct((PEER_HEADS, N_KEYS, n), F32)] * 4,
        scratch_shapes=[pltpu.VMEM((N_KEYS, tr), F32), pltpu.VMEM((N_KEYS, tr), F32),
                        pltpu.VMEM((N_KEYS, tr), F32), pltpu.VMEM((PEER_TOPK, tr), F32),
                        pltpu.VMEM((PEER_TOPK, tr), F32), pltpu.VMEM((ncand, tr), F32),
                        pltpu.VMEM((ncand, tr), F32), pltpu.VMEM((PEER_TOPK, tr), F32)],
        compiler_params=_params("parallel", "arbitrary"),
    )(qp, keys)


def _gelu(x):
    return 0.5 * x * (1.0 + lax.erf(x * (2.0 ** -0.5)))


def _experts_kernel(x_ref, h_ref, lf_ref, e1_ref, r2_ref, e2_ref, u_ref, v_ref, gfin_ref, o_ref, acc_ref,
                    *, et):
    e = pl.program_id(1)

    @pl.when(e == 0)
    def _():
        acc_ref[...] = jnp.zeros_like(acc_ref)

    s = _dot_nt(u_ref[...], x_ref[...])
    act = _gelu(s)
    gates = []
    for sub in range(et // N_KEYS):
        i = e * (et // N_KEYS) + sub
        w = None
        for h in range(PEER_HEADS):
            lrow = lf_ref[h, pl.ds(i, 1), :]
            erow = e1_ref[h, pl.ds(i, 1), :]
            t = jnp.where(r2_ref[h] < lrow, e2_ref[h], 0.0) * erow
            w = t if w is None else w + t
        gates.append(w)
    a = (jnp.concatenate(gates, axis=0) * act).astype(BF16)
    acc_ref[...] += _dot_tn(a, v_ref[...])

    @pl.when(e == pl.num_programs(1) - 1)
    def _():
        o_ref[...] = _rms(h_ref[...] + acc_ref[...], gfin_ref[...])


def _experts(xn, h2, tabs, u_bf, v_bf, g_final, tp, et):
    n, d = xn.shape
    tab = pl.BlockSpec((PEER_HEADS, N_KEYS, tp), lambda t, e: (0, 0, t))
    row = pl.BlockSpec((tp, d), lambda t, e: (t, 0))
    wt = pl.BlockSpec((et, d), lambda t, e: (e, 0))
    return pl.pallas_call(
        functools.partial(_experts_kernel, et=et),
        grid=(n // tp, N_EXPERTS // et),
        in_specs=[row, row, tab, tab, tab, tab, wt, wt, _resident((1, d))],
        out_specs=row,
        out_shape=jax.ShapeDtypeStruct((n, d), F32),
        scratch_shapes=[pltpu.VMEM((tp, d), F32)],
        compiler_params=_params("parallel", "arbitrary"),
    )(xn, h2, *tabs, u_bf, v_bf, g_final.reshape(1, d))


def _layer(x, pos_offset, conv_hist, k_hist, v_hist, mk, mv, wts, g_final, tb, tp):
    (g_mix, w_in, dw_w, dw_b, ln_g, ln_b, sinks, w_out, g_cross, w_cq, w_co, g_ffn, wq, keys, u, v) = wts
    b, t, d = x.shape
    glu, q, k, vv = _in_proj(x, g_mix, w_in, pos_offset, tb)
    if conv_hist is None:
        conv_hist = jnp.zeros((b, HIST, D_CONV), F32)
    conv_out = _conv(glu, conv_hist, dw_w, dw_b, ln_g, ln_b, tb)
    if k_hist is None:
        attn = _attn_prompt(q, k, vv, sinks)
        new_k, new_v = k[:, -WINDOW:], vv[:, -WINDOW:]
    else:
        attn = _attn_cached(q, k_hist, v_hist, k, vv, sinks)
        new_k = jnp.concatenate([k_hist, k], axis=1)[:, -WINDOW:]
        new_v = jnp.concatenate([v_hist, vv], axis=1)[:, -WINDOW:]
    h2, xn, qp = _mid(x, conv_out, attn, mk, mv, w_out, g_cross, w_cq, w_co, g_ffn, wq, tb)
    n = b * t
    tabs = _router(qp.reshape(n, -1), keys, min(256, n))
    y = _experts(xn.reshape(n, d), h2.reshape(n, d), tabs, u, v, g_final, tp, 256)
    new_conv = glu[:, -HIST:]
    return (y.reshape(b, t, d), new_conv,
            new_k.reshape(b, WINDOW, N_KV_HEADS, HEAD_DIM), new_v.reshape(b, WINDOW, N_KV_HEADS, HEAD_DIM))


def kernel(x_prompt, x_sample, mem_prompt, cache_conv, cache_swa_k, cache_swa_v, cache_mem_k, cache_mem_v, norm_mix_g, w_in, conv_dw_w, conv_dw_b, conv_ln_g, conv_ln_b, swa_sinks, w_out, norm_cross_g, norm_mem_g, w_cq, w_ck, w_cv, w_co, norm_ffn_g, peer_wq, peer_subkeys, peer_u, peer_v, norm_final_g):
    depth = w_in.shape[0]
    assert depth == 1, "the final rmsnorm is fused into the last layer's expert kernel"
    bs = x_sample.shape[0]
    l = 0
    bf = lambda a: a.astype(BF16)
    wts = (norm_mix_g[l], bf(w_in[l]), conv_dw_w[l], conv_dw_b[l], conv_ln_g[l], conv_ln_b[l], swa_sinks[l],
           bf(w_out[l]), norm_cross_g[l], bf(w_cq[l]), bf(w_co[l]), norm_ffn_g[l], bf(peer_wq[l]),
           peer_subkeys[l], bf(peer_u[l]), bf(peer_v[l]))
    mk, mv = _mem_kv(mem_prompt, norm_mem_g[l], bf(w_ck[l]), bf(w_cv[l]))
    yp, conv_p, k_p, v_p = _layer(x_prompt, 0, None, None, None, mk, mv, wts, norm_final_g,
                                  tb=256, tp=512)
    ys, conv_s, k_s, v_s = _layer(
        x_sample, PAST_LEN, cache_conv[l], cache_swa_k[l].reshape(bs, WINDOW, KV_DIM),
        cache_swa_v[l].reshape(bs, WINDOW, KV_DIM), cache_mem_k[l].reshape(bs, N_MEM, D_MEM),
        cache_mem_v[l].reshape(bs, N_MEM, D_MEM), wts, norm_final_g, tb=x_sample.shape[1], tp=256)
    bp = x_prompt.shape[0]
    mem_shape = (1, bp, N_MEM, MEM_HEADS, MEM_HEAD_DIM)
    return (yp, ys, conv_p[None], k_p[None], v_p[None], mk.reshape(mem_shape), mv.reshape(mem_shape),
            conv_s[None], k_s[None], v_s[None])
```

```python
import functools

import jax
import jax.numpy as jnp
from jax import lax
from jax.experimental import pallas as pl
from jax.experimental.pallas import tpu as pltpu

F32 = jnp.float32
BF16 = jnp.bfloat16

D_MODEL = 2048
PAST_LEN = 1024
CHUNK = 64
D_CONV = 1024
CONV_WIDTH = 31
HIST = CONV_WIDTH - 1
HIST_PAD = 32
N_HEADS = 16
N_KV_HEADS = 4
HEAD_DIM = 64
GROUP = N_HEADS // N_KV_HEADS
D_ATTN = N_HEADS * HEAD_DIM
KV_DIM = N_KV_HEADS * HEAD_DIM
IN_COLS = 2 * D_CONV + D_ATTN + 2 * KV_DIM
ROT_DIM = HEAD_DIM // 4
ROT_HALF = ROT_DIM // 2
ROPE_THETA = 500000.0
WINDOW = 128
WINDOW_CHUNKS = WINDOW // CHUNK
N_MEM = 256
MEM_HEADS = 4
MEM_HEAD_DIM = 128
D_MEM = MEM_HEADS * MEM_HEAD_DIM
PEER_HEADS = 8
N_KEYS = 128
N_EXPERTS = N_KEYS * N_KEYS
PEER_DK = 256
PEER_TOPK = 16
EPS = 1e-6
NEG = -1e30

LANES = 128
SUBLANES = 8
VMEM_LIMIT = 56 * 1024 * 1024


def _params(*sem):
    return pltpu.CompilerParams(dimension_semantics=sem, vmem_limit_bytes=VMEM_LIMIT)


def _resident(shape):
    return pl.BlockSpec(shape, lambda *_: (0,) * len(shape), pipeline_mode=pl.Buffered(1))


def _rms(x, g):
    return x * lax.rsqrt(jnp.mean(x * x, axis=-1, keepdims=True) + EPS) * g


def _dot(a, b):
    return jnp.dot(a, b, preferred_element_type=F32)


def _dot_nt(a, b):
    return lax.dot_general(a, b, (((1,), (1,)), ((), ())), preferred_element_type=F32)


def _dot_tn(a, b):
    return lax.dot_general(a, b, (((0,), (0,)), ((), ())), preferred_element_type=F32)


def _rope(x, cos, sin_up, sin_dn):
    return (x * cos + pltpu.roll(x, ROT_HALF, 1) * sin_up
            + pltpu.roll(x, LANES - ROT_HALF, 1) * sin_dn)


def _in_proj_kernel(x_ref, g_ref, w_ref, cos_ref, sup_ref, sdn_ref, glu_ref, q_ref, k_ref, v_ref):
    xn = _rms(x_ref[...], g_ref[...]).astype(BF16)
    a1 = _dot(xn, w_ref[:, 0:D_CONV])
    a2 = _dot(xn, w_ref[:, D_CONV:2 * D_CONV])
    glu_ref[...] = a1 / (1.0 + jnp.exp(-a2))
    cos, sup, sdn = cos_ref[...], sup_ref[...], sdn_ref[...]
    c0 = 2 * D_CONV
    for c in range(D_ATTN // 256):
        qc = _dot(xn, w_ref[:, c0 + 256 * c:c0 + 256 * (c + 1)])
        q_ref[:, 256 * c:256 * c + 128] = _rope(qc[:, 0:128], cos, sup, sdn)
        q_ref[:, 256 * c + 128:256 * (c + 1)] = _rope(qc[:, 128:256], cos, sup, sdn)
    c1 = c0 + D_ATTN
    kc = _dot(xn, w_ref[:, c1:c1 + KV_DIM])
    k_ref[:, 0:128] = _rope(kc[:, 0:128], cos, sup, sdn)
    k_ref[:, 128:256] = _rope(kc[:, 128:256], cos, sup, sdn)
    v_ref[...] = _dot(xn, w_ref[:, c1 + KV_DIM:c1 + 2 * KV_DIM])


def _rope_tables(t, offset):
    pos = (offset + jnp.arange(t, dtype=jnp.int32)).astype(F32)
    inv_freq = jnp.power(jnp.float32(ROPE_THETA), -(2.0 * jnp.arange(ROT_HALF, dtype=F32)) / ROT_DIM)
    ang = pos[:, None] * inv_freq[None, :]
    d = jnp.arange(LANES) % HEAD_DIM
    ang_l = ang[:, d % ROT_HALF]
    cos = jnp.where(d[None, :] < ROT_DIM, jnp.cos(ang_l), 1.0)
    sin = jnp.sin(ang_l)
    sin_up = jnp.where((d[None, :] >= ROT_HALF) & (d[None, :] < ROT_DIM), sin, 0.0)
    sin_dn = jnp.where(d[None, :] < ROT_HALF, -sin, 0.0)
    return cos.astype(F32), sin_up.astype(F32), sin_dn.astype(F32)


def _in_proj(x, g, w_bf, pos_offset, tb):
    b, t, d = x.shape
    cos, sup, sdn = _rope_tables(t, pos_offset)
    row = lambda c: pl.BlockSpec((None, tb, c), lambda i, j: (i, j, 0))
    tab = pl.BlockSpec((tb, LANES), lambda i, j: (j, 0))
    return pl.pallas_call(
        _in_proj_kernel,
        grid=(b, t // tb),
        in_specs=[row(d), _resident((1, d)), _resident((d, IN_COLS)), tab, tab, tab],
        out_specs=[row(D_CONV), row(D_ATTN), row(KV_DIM), row(KV_DIM)],
        out_shape=[jax.ShapeDtypeStruct((b, t, c), F32) for c in (D_CONV, D_ATTN, KV_DIM, KV_DIM)],
        compiler_params=_params("parallel", "arbitrary"),
    )(x, g.reshape(1, d), w_bf, cos, sup, sdn)


def _conv_kernel(glu_ref, hist_ref, w_ref, b_ref, lg_ref, lb_ref, o_ref, xin_ref, y_ref, *, tb):
    @pl.when(pl.program_id(1) == 0)
    def _():
        xin_ref[0:HIST_PAD, :] = hist_ref[...]

    xin_ref[HIST_PAD:HIST_PAD + tb, :] = glu_ref[...]
    lead = HIST_PAD - HIST
    for c in range(D_CONV // LANES):
        cs = slice(c * LANES, (c + 1) * LANES)
        acc = jnp.zeros((tb, LANES), F32)
        for k in range(CONV_WIDTH):
            acc = acc + xin_ref[lead + k:lead + k + tb, cs] * w_ref[k:k + 1, cs]
        y_ref[:, cs] = acc + b_ref[:, cs]
    y = y_ref[...]
    mu = jnp.mean(y, axis=-1, keepdims=True)
    yc = y - mu
    var = jnp.mean(yc * yc, axis=-1, keepdims=True)
    z = yc * lax.rsqrt(var + EPS) * lg_ref[...] + lb_ref[...]
    o_ref[...] = z / (1.0 + jnp.exp(-z))
    xin_ref[0:HIST_PAD, :] = xin_ref[tb:tb + HIST_PAD, :]


def _conv(glu, hist, w, bias, ln_g, ln_b, tb):
    b, t, c = glu.shape
    hist_p = jnp.pad(hist.astype(F32), ((0, 0), (HIST_PAD - HIST, 0), (0, 0)))
    w_p = jnp.pad(w, ((0, HIST_PAD - CONV_WIDTH), (0, 0)))
    row = pl.BlockSpec((None, tb, c), lambda i, j: (i, j, 0))
    return pl.pallas_call(
        functools.partial(_conv_kernel, tb=tb),
        grid=(b, t // tb),
        in_specs=[row, pl.BlockSpec((None, HIST_PAD, c), lambda i, j: (i, 0, 0)),
                  _resident((HIST_PAD, c)), _resident((1, c)), _resident((1, c)), _resident((1, c))],
        out_specs=row,
        out_shape=jax.ShapeDtypeStruct((b, t, c), F32),
        scratch_shapes=[pltpu.VMEM((tb + HIST_PAD, c), F32), pltpu.VMEM((tb, c), F32)],
        compiler_params=_params("parallel", "arbitrary"),
    )(glu, hist_p, w_p, bias.reshape(1, c), ln_g.reshape(1, c), ln_b.reshape(1, c))


Q_BLOCK = 2 * CHUNK


def _attn_banded_kernel(sinks_ref, q_ref, *refs):
    n_kb = Q_BLOCK // CHUNK + WINDOW_CHUNKS
    k_refs, v_refs, o_ref = refs[:n_kb], refs[n_kb:2 * n_kb], refs[2 * n_kb]
    q = q_ref[...].astype(BF16)
    kk = jnp.concatenate([r[...] for r in k_refs], axis=0).astype(BF16)
    vv = jnp.concatenate([r[...] for r in v_refs], axis=0).astype(BF16)
    tk = n_kb * CHUNK
    key_chunk = lax.broadcasted_iota(jnp.int32, (tk, Q_BLOCK), 0) // CHUNK
    qry_chunk = lax.broadcasted_iota(jnp.int32, (tk, Q_BLOCK), 1) // CHUNK
    back = qry_chunk + WINDOW_CHUNKS - key_chunk
    first = pl.program_id(1) * (Q_BLOCK // CHUNK) - WINDOW_CHUNKS
    valid = (back >= 0) & (back <= WINDOW_CHUNKS) & (first + key_chunk >= 0)
    pad = jnp.zeros((tk, HEAD_DIM), BF16)
    for g in range(N_KV_HEADS):
        kg = kk[:, g * HEAD_DIM:(g + 1) * HEAD_DIM]
        vg = vv[:, g * HEAD_DIM:(g + 1) * HEAD_DIM]
        k_pad = (jnp.concatenate([kg, pad], axis=1), jnp.concatenate([pad, kg], axis=1))
        v_pad = (jnp.concatenate([vg, pad], axis=1), jnp.concatenate([pad, vg], axis=1))
        for pr in range(GROUP // 2):
            cols = slice((g * GROUP // 2 + pr) * LANES, (g * GROUP // 2 + pr + 1) * LANES)
            qb = q[:, cols]
            acc = None
            for side in range(2):
                sink = sinks_ref[g * GROUP + 2 * pr + side]
                s = _dot_nt(k_pad[side], qb) * (HEAD_DIM ** -0.5)
                s = jnp.where(valid, s, NEG)
                m = jnp.maximum(jnp.max(s, axis=0, keepdims=True), sink)
                p = jnp.exp(s - m)
                den = jnp.sum(p, axis=0, keepdims=True) + jnp.exp(sink - m)
                o = _dot_tn((p / den).astype(BF16), v_pad[side])
                acc = o if acc is None else acc + o
            o_ref[:, cols] = acc


def _attn_prompt(q, k, v, sinks):
    b, t, _ = q.shape
    per = Q_BLOCK // CHUNK
    n_kb = per + WINDOW_CHUNKS
    qspec = pl.BlockSpec((None, Q_BLOCK, D_ATTN), lambda i, c: (i, c, 0))

    def kspec(j):
        return pl.BlockSpec((None, CHUNK, KV_DIM),
                            lambda i, c: (i, jnp.maximum(c * per + j - WINDOW_CHUNKS, 0), 0))

    kspecs = [kspec(j) for j in range(n_kb)]
    return pl.pallas_call(
        _attn_banded_kernel,
        grid=(b, t // Q_BLOCK),
        in_specs=[pl.BlockSpec(memory_space=pltpu.SMEM), qspec] + kspecs + kspecs,
        out_specs=qspec,
        out_shape=jax.ShapeDtypeStruct((b, t, D_ATTN), F32),
        compiler_params=_params("parallel", "arbitrary"),
    )(sinks, q, *([k] * n_kb), *([v] * n_kb))


def _attn_cached_kernel(sinks_ref, q_ref, kh_ref, kn_ref, vh_ref, vn_ref, o_ref):
    q = q_ref[...]
    tq = q.shape[0]
    kk = jnp.concatenate([kh_ref[...], kn_ref[...]], axis=0).astype(BF16)
    vv = jnp.concatenate([vh_ref[...], vn_ref[...]], axis=0).astype(BF16)
    outs = []
    for g in range(N_KV_HEADS):
        heads = range(g * GROUP, (g + 1) * GROUP)
        qg = jnp.concatenate([q[:, h * HEAD_DIM:(h + 1) * HEAD_DIM] for h in heads], axis=0)
        kg = kk[:, g * HEAD_DIM:(g + 1) * HEAD_DIM]
        vg = vv[:, g * HEAD_DIM:(g + 1) * HEAD_DIM]
        s = _dot_nt(qg.astype(BF16), kg) * (HEAD_DIM ** -0.5)
        sink = jnp.concatenate([jnp.full((tq, 1), sinks_ref[h], F32) for h in heads], axis=0)
        m = jnp.maximum(jnp.max(s, axis=-1, keepdims=True), sink)
        p = jnp.exp(s - m)
        den = jnp.sum(p, axis=-1, keepdims=True) + jnp.exp(sink - m)
        og = _dot((p / den).astype(BF16), vg)
        outs += [og[i * tq:(i + 1) * tq, :] for i in range(GROUP)]
    o_ref[...] = jnp.concatenate(outs, axis=1)


def _attn_cached(q, k_hist, v_hist, k, v, sinks):
    b, t, _ = q.shape
    th = k_hist.shape[1]
    qspec = pl.BlockSpec((None, t, D_ATTN), lambda i: (i, 0, 0))
    hspec = pl.BlockSpec((None, th, KV_DIM), lambda i: (i, 0, 0))
    nspec = pl.BlockSpec((None, t, KV_DIM), lambda i: (i, 0, 0))
    return pl.pallas_call(
        _attn_cached_kernel,
        grid=(b,),
        in_specs=[pl.BlockSpec(memory_space=pltpu.SMEM), qspec, hspec, nspec, hspec, nspec],
        out_specs=qspec,
        out_shape=jax.ShapeDtypeStruct((b, t, D_ATTN), F32),
        compiler_params=_params("parallel"),
    )(sinks, q, k_hist, k, v_hist, v)


def _mem_kv_kernel(m_ref, g_ref, wk_ref, wv_ref, k_ref, v_ref):
    mn = _rms(m_ref[...], g_ref[...]).astype(BF16)
    k_ref[...] = _dot(mn, wk_ref[...])
    v_ref[...] = _dot(mn, wv_ref[...])


def _mem_kv(mem, g, wk_bf, wv_bf):
    b, n, d = mem.shape
    out = pl.BlockSpec((None, n, D_MEM), lambda i: (i, 0, 0))
    return pl.pallas_call(
        _mem_kv_kernel,
        grid=(b,),
        in_specs=[pl.BlockSpec((None, n, d), lambda i: (i, 0, 0)), _resident((1, d)),
                  _resident((d, D_MEM)), _resident((d, D_MEM))],
        out_specs=[out, out],
        out_shape=[jax.ShapeDtypeStruct((b, n, D_MEM), F32)] * 2,
        compiler_params=_params("parallel"),
    )(mem, g.reshape(1, d), wk_bf, wv_bf)


def _mid_kernel(x_ref, conv_ref, attn_ref, wout_ref, gc_ref, wcq_ref, mk_ref, mv_ref, wco_ref,
                gf_ref, wq_ref, h2_ref, xn_ref, qp_ref):
    h1 = (x_ref[...] + _dot(conv_ref[...].astype(BF16), wout_ref[0:D_CONV, :])
          + _dot(attn_ref[...].astype(BF16), wout_ref[D_CONV:D_CONV + D_ATTN, :]))
    qc = _dot(_rms(h1, gc_ref[...]).astype(BF16), wcq_ref[...])
    mk = mk_ref[...].astype(BF16)
    mv = mv_ref[...].astype(BF16)
    outs = []
    for h in range(MEM_HEADS):
        hs = slice(h * MEM_HEAD_DIM, (h + 1) * MEM_HEAD_DIM)
        s = _dot_nt(qc[:, hs].astype(BF16), mk[:, hs]) * (MEM_HEAD_DIM ** -0.5)
        p = jnp.exp(s - jnp.max(s, axis=-1, keepdims=True))
        p = p / jnp.sum(p, axis=-1, keepdims=True)
        outs.append(_dot(p.astype(BF16), mv[:, hs]))
    o = jnp.concatenate(outs, axis=1).astype(BF16)
    h2 = h1 + _dot(o, wco_ref[...])
    h2_ref[...] = h2
    xn = _rms(h2, gf_ref[...]).astype(BF16)
    xn_ref[...] = xn
    qp_ref[...] = _dot(xn, wq_ref[...])


def _mid(x, conv_out, attn_out, mk, mv, wout_bf, gc, wcq_bf, wco_bf, gf, wq_bf, tb):
    b, t, d = x.shape
    row = lambda c: pl.BlockSpec((None, tb, c), lambda i, j: (i, j, 0))
    mem = pl.BlockSpec((None, N_MEM, D_MEM), lambda i, j: (i, 0, 0))
    dq = wq_bf.shape[1]
    return pl.pallas_call(
        _mid_kernel,
        grid=(b, t // tb),
        in_specs=[row(d), row(D_CONV), row(D_ATTN), _resident((D_CONV + D_ATTN, d)), _resident((1, d)),
                  _resident((d, D_MEM)), mem, mem, _resident((D_MEM, d)), _resident((1, d)),
                  _resident((d, dq))],
        out_specs=[row(d), row(d), row(dq)],
        out_shape=[jax.ShapeDtypeStruct((b, t, d), F32), jax.ShapeDtypeStruct((b, t, d), BF16),
                   jax.ShapeDtypeStruct((b, t, dq), F32)],
        compiler_params=_params("parallel", "arbitrary"),
    )(x, conv_out, attn_out, wout_bf, gc.reshape(1, d), wcq_bf, mk, mv, wco_bf, gf.reshape(1, d), wq_bf)


CAND_WIDE = PEER_TOPK // 2
CAND_ROWS = PEER_TOPK + (CAND_WIDE - 1) * CAND_WIDE + CAND_WIDE


def _topk_rank(jobs):
    for w_ref, r_ref, _, _ in jobs:
        r_ref[...] = jnp.full(w_ref.shape, float(PEER_TOPK), F32)

    def body(k, carry):
        for w_ref, r_ref, sv_ref, rows_ref in jobs:
            w = w_ref[...]
            rows = rows_ref[...]
            m = jnp.max(w, axis=0, keepdims=True)
            first = jnp.min(jnp.where(w == m, rows, float(w.shape[0])), axis=0, keepdims=True)
            hit = rows == first
            sv_ref[pl.ds(k, 1), :] = m
            r_ref[...] = jnp.where(hit, k.astype(F32), r_ref[...])
            w_ref[...] = jnp.where(hit, -jnp.inf, w)
        return carry

    lax.fori_loop(0, PEER_TOPK, body, 0)


def _router_kernel(q_ref, keys_ref, lf_ref, e1_ref, r2_ref, e2_ref,
                   w1_ref, w2_ref, r1_ref, r2f_ref, sv1_ref, sv2_ref, cand_ref, rc_ref, svc_ref,
                   rows_ref, crow_ref):
    q = q_ref[...]
    half = PEER_DK // 2
    tr = q.shape[0]
    s1 = _dot_nt(keys_ref[0], q[:, 0:half])
    s2 = _dot_nt(keys_ref[1], q[:, half:PEER_DK])
    w1_ref[...] = s1
    w2_ref[...] = s2
    rows_ref[...] = lax.broadcasted_iota(jnp.int32, (N_KEYS, tr), 0).astype(F32)
    crow_ref[...] = lax.broadcasted_iota(jnp.int32, (CAND_ROWS, tr), 0).astype(F32)
    _topk_rank([(w1_ref, r1_ref, sv1_ref, rows_ref), (w2_ref, r2f_ref, sv2_ref, rows_ref)])
    cand_ref[0:PEER_TOPK, :] = sv1_ref[0:1, :] + sv2_ref[...]
    for k1 in range(1, CAND_WIDE):
        r0 = CAND_WIDE * (k1 + 1)
        cand_ref[r0:r0 + CAND_WIDE, :] = sv1_ref[k1:k1 + 1, :] + sv2_ref[0:CAND_WIDE, :]
    tail = CAND_ROWS - CAND_WIDE
    cand_ref[tail:CAND_ROWS, :] = sv1_ref[CAND_WIDE:PEER_TOPK, :] + sv2_ref[0:1, :]
    _topk_rank([(cand_ref, rc_ref, svc_ref, crow_ref)])
    g1 = jnp.exp(sv1_ref[...] - sv1_ref[0:1, :])
    g2 = jnp.exp(sv2_ref[...] - sv2_ref[0:1, :])
    sel = jnp.where(rc_ref[...] < float(PEER_TOPK), 1.0, 0.0)
    r1 = r1_ref[...]
    blk = sel[0:PEER_TOPK, :]
    z = jnp.sum(blk * g2, axis=0, keepdims=True)
    lf = jnp.where(r1 == 0.0, jnp.sum(blk, axis=0, keepdims=True), 0.0)
    for k1 in range(1, CAND_WIDE):
        r0 = CAND_WIDE * (k1 + 1)
        blk = sel[r0:r0 + CAND_WIDE, :]
        z = z + g1[k1:k1 + 1, :] * jnp.sum(blk * g2[0:CAND_WIDE, :], axis=0, keepdims=True)
        lf = lf + jnp.where(r1 == float(k1), jnp.sum(blk, axis=0, keepdims=True), 0.0)
    blk = sel[tail:CAND_ROWS, :]
    z = z + jnp.sum(blk * g1[CAND_WIDE:PEER_TOPK, :], axis=0, keepdims=True)
    for r in range(CAND_WIDE):
        lf = lf + jnp.where(r1 == float(CAND_WIDE + r), blk[r:r + 1, :], 0.0)
    lf_ref[...] = lf
    e1_ref[...] = jnp.exp(s1 - sv1_ref[0:1, :]) / z
    r2_ref[...] = pltpu.bitcast(r2f_ref[...].astype(BF16), jnp.uint32)
    e2_ref[...] = pltpu.bitcast(jnp.exp(s2 - sv2_ref[0:1, :]).astype(BF16), jnp.uint32)


def _router(qp, keys, tr):
    n = qp.shape[0]
    tab = pl.BlockSpec((None, N_KEYS, tr), lambda t, h: (h, 0, t))
    ptab = pl.BlockSpec((None, N_KEYS // 2, tr), lambda t, h: (h, 0, t))
    f32_tab = jax.ShapeDtypeStruct((PEER_HEADS, N_KEYS, n), F32)
    packed_tab = jax.ShapeDtypeStruct((PEER_HEADS, N_KEYS // 2, n), jnp.uint32)
    vm = lambda rows: pltpu.VMEM((rows, tr), F32)
    return pl.pallas_call(
        _router_kernel,
        grid=(n // tr, PEER_HEADS),
        in_specs=[pl.BlockSpec((tr, PEER_DK), lambda t, h: (t, h)),
                  pl.BlockSpec((None, 2, N_KEYS, PEER_DK // 2), lambda t, h: (h, 0, 0, 0))],
        out_specs=[tab, tab, ptab, ptab],
        out_shape=[f32_tab, f32_tab, packed_tab, packed_tab],
        scratch_shapes=[vm(N_KEYS), vm(N_KEYS), vm(N_KEYS), vm(N_KEYS), vm(PEER_TOPK), vm(PEER_TOPK),
                        vm(CAND_ROWS), vm(CAND_ROWS), vm(PEER_TOPK), vm(N_KEYS), vm(CAND_ROWS)],
        compiler_params=_params("parallel", "arbitrary"),
    )(qp, keys)


def _gelu(x):
    return 0.5 * x * (1.0 + lax.erf(x * (2.0 ** -0.5)))


PACK = 16


def _experts_kernel(x_ref, h_ref, lf_ref, e1_ref, r2_ref, e2_ref, u_ref, v_ref, gfin_ref, o_ref, s_ref, a_ref,
                    *, et):
    e = pl.program_id(1)
    tp = x_ref.shape[0]

    @pl.when(e == 0)
    def _():
        o_ref[...] = h_ref[...]

    s_ref[...] = _dot_nt(u_ref[...], x_ref[...])
    zero = jnp.zeros((PACK, LANES), BF16)
    n_sub = et // N_KEYS
    i0 = pl.multiple_of(e * n_sub, n_sub)
    for tb in range(tp // LANES):
        ls = slice(tb * LANES, (tb + 1) * LANES)
        lf_i = [lf_ref[h, pl.ds(i0, n_sub), ls] for h in range(PEER_HEADS)]
        e1_i = [e1_ref[h, pl.ds(i0, n_sub), ls] for h in range(PEER_HEADS)]
        for sub in range(n_sub):
            lrows = [jnp.broadcast_to(t[sub:sub + 1, :], (PACK, LANES)).astype(BF16) for t in lf_i]
            erows = [jnp.broadcast_to(t[sub:sub + 1, :], (PACK, LANES)).astype(BF16) for t in e1_i]
            for c in range(N_KEYS // PACK):
                js = slice(c * SUBLANES, (c + 1) * SUBLANES)
                rs = slice(sub * N_KEYS + c * PACK, sub * N_KEYS + (c + 1) * PACK)
                w = None
                for h in range(PEER_HEADS):
                    r2 = pltpu.bitcast(r2_ref[h, js, ls], BF16)
                    e2 = pltpu.bitcast(e2_ref[h, js, ls], BF16)
                    t = jnp.where(r2 < lrows[h], e2, zero) * erows[h]
                    w = t if w is None else w + t
                a_ref[rs, ls] = w * _gelu(s_ref[rs, ls]).astype(BF16)
    o_ref[...] += _dot_tn(a_ref[...], v_ref[...])

    @pl.when(e == pl.num_programs(1) - 1)
    def _():
        o_ref[...] = _rms(o_ref[...], gfin_ref[...])


def _experts(xn, h2, tabs, u_bf, v_bf, g_final, tp, et):
    n, d = xn.shape
    assert et % (SUBLANES * N_KEYS) == 0, "a step's first-half keys must be whole sublane tiles of the gate tables"
    tab = pl.BlockSpec((PEER_HEADS, N_KEYS, tp), lambda t, e: (0, 0, t))
    ptab = pl.BlockSpec((PEER_HEADS, N_KEYS // 2, tp), lambda t, e: (0, 0, t))
    row = pl.BlockSpec((tp, d), lambda t, e: (t, 0))
    wt = pl.BlockSpec((et, d), lambda t, e: (e, 0))
    return pl.pallas_call(
        functools.partial(_experts_kernel, et=et),
        grid=(n // tp, N_EXPERTS // et),
        in_specs=[row, row, tab, tab, ptab, ptab, wt, wt, _resident((1, d))],
        out_specs=row,
        out_shape=jax.ShapeDtypeStruct((n, d), F32),
        scratch_shapes=[pltpu.VMEM((et, tp), F32), pltpu.VMEM((et, tp), BF16)],
        compiler_params=_params("parallel", "arbitrary"),
    )(xn, h2, *tabs, u_bf, v_bf, g_final.reshape(1, d))


def _layer(x, pos_offset, conv_hist, k_hist, v_hist, mk, mv, wts, g_final, tb, tp):
    (g_mix, w_in, dw_w, dw_b, ln_g, ln_b, sinks, w_out, g_cross, w_cq, w_co, g_ffn, wq, keys, u, v) = wts
    b, t, d = x.shape
    glu, q, k, vv = _in_proj(x, g_mix, w_in, pos_offset, tb)
    if conv_hist is None:
        conv_hist = jnp.zeros((b, HIST, D_CONV), F32)
    conv_out = _conv(glu, conv_hist, dw_w, dw_b, ln_g, ln_b, tb)
    if k_hist is None:
        attn = _attn_prompt(q, k, vv, sinks)
        new_k, new_v = k[:, -WINDOW:], vv[:, -WINDOW:]
    else:
        attn = _attn_cached(q, k_hist, v_hist, k, vv, sinks)
        new_k = jnp.concatenate([k_hist, k], axis=1)[:, -WINDOW:]
        new_v = jnp.concatenate([v_hist, vv], axis=1)[:, -WINDOW:]
    h2, xn, qp = _mid(x, conv_out, attn, mk, mv, w_out, g_cross, w_cq, w_co, g_ffn, wq, tb)
    n = b * t
    tabs = _router(qp.reshape(n, -1), keys, min(256, n))
    y = _experts(xn.reshape(n, d), h2.reshape(n, d), tabs, u, v, g_final, tp, 1024)
    new_conv = glu[:, -HIST:]
    return (y.reshape(b, t, d), new_conv,
            new_k.reshape(b, WINDOW, N_KV_HEADS, HEAD_DIM), new_v.reshape(b, WINDOW, N_KV_HEADS, HEAD_DIM))


def kernel(x_prompt, x_sample, mem_prompt, cache_conv, cache_swa_k, cache_swa_v, cache_mem_k, cache_mem_v, norm_mix_g, w_in, conv_dw_w, conv_dw_b, conv_ln_g, conv_ln_b, swa_sinks, w_out, norm_cross_g, norm_mem_g, w_cq, w_ck, w_cv, w_co, norm_ffn_g, peer_wq, peer_subkeys, peer_u, peer_v, norm_final_g):
    depth = w_in.shape[0]
    assert depth == 1, "the final rmsnorm is fused into the last layer's expert kernel"
    bs = x_sample.shape[0]
    l = 0
    bf = lambda a: a.astype(BF16)
    wts = (norm_mix_g[l], bf(w_in[l]), conv_dw_w[l], conv_dw_b[l], conv_ln_g[l], conv_ln_b[l], swa_sinks[l],
           bf(w_out[l]), norm_cross_g[l], bf(w_cq[l]), bf(w_co[l]), norm_ffn_g[l], bf(peer_wq[l]),
           peer_subkeys[l], bf(peer_u[l]), bf(peer_v[l]))
    mk, mv = _mem_kv(mem_prompt, norm_mem_g[l], bf(w_ck[l]), bf(w_cv[l]))
    yp, conv_p, k_p, v_p = _layer(x_prompt, 0, None, None, None, mk, mv, wts, norm_final_g,
                                  tb=256, tp=512)
    ys, conv_s, k_s, v_s = _layer(
        x_sample, PAST_LEN, cache_conv[l], cache_swa_k[l].reshape(bs, WINDOW, KV_DIM),
        cache_swa_v[l].reshape(bs, WINDOW, KV_DIM), cache_mem_k[l].reshape(bs, N_MEM, D_MEM),
        cache_mem_v[l].reshape(bs, N_MEM, D_MEM), wts, norm_final_g, tb=x_sample.shape[1], tp=256)
    bp = x_prompt.shape[0]
    mem_shape = (1, bp, N_MEM, MEM_HEADS, MEM_HEAD_DIM)
    return (yp, ys, conv_p[None], k_p[None], v_p[None], mk.reshape(mem_shape), mv.reshape(mem_shape),
            conv_s[None], k_s[None], v_s[None])
```

```python
import functools

import jax
import jax.numpy as jnp
from jax import lax
from jax.experimental import pallas as pl
from jax.experimental.pallas import tpu as pltpu

F32 = jnp.float32
BF16 = jnp.bfloat16

D_MODEL = 2048
PAST_LEN = 1024
CHUNK = 64
D_CONV = 1024
CONV_WIDTH = 31
HIST = CONV_WIDTH - 1
HIST_PAD = 32
N_HEADS = 16
N_KV_HEADS = 4
HEAD_DIM = 64
GROUP = N_HEADS // N_KV_HEADS
D_ATTN = N_HEADS * HEAD_DIM
KV_DIM = N_KV_HEADS * HEAD_DIM
IN_COLS = 2 * D_CONV + D_ATTN + 2 * KV_DIM
ROT_DIM = HEAD_DIM // 4
ROT_HALF = ROT_DIM // 2
ROPE_THETA = 500000.0
WINDOW = 128
WINDOW_CHUNKS = WINDOW // CHUNK
N_MEM = 256
MEM_HEADS = 4
MEM_HEAD_DIM = 128
D_MEM = MEM_HEADS * MEM_HEAD_DIM
PEER_HEADS = 8
N_KEYS = 128
N_EXPERTS = N_KEYS * N_KEYS
PEER_DK = 256
PEER_TOPK = 16
EPS = 1e-6
NEG = -1e30

LANES = 128
SUBLANES = 8
VMEM_LIMIT = 56 * 1024 * 1024


def _params(*sem):
    return pltpu.CompilerParams(dimension_semantics=sem, vmem_limit_bytes=VMEM_LIMIT)


def _resident(shape):
    return pl.BlockSpec(shape, lambda *_: (0,) * len(shape), pipeline_mode=pl.Buffered(1))


def _rms(x, g):
    return x * lax.rsqrt(jnp.mean(x * x, axis=-1, keepdims=True) + EPS) * g


def _dot(a, b):
    return jnp.dot(a, b, preferred_element_type=F32)


def _dot_nt(a, b):
    return lax.dot_general(a, b, (((1,), (1,)), ((), ())), preferred_element_type=F32)


def _dot_tn(a, b):
    return lax.dot_general(a, b, (((0,), (0,)), ((), ())), preferred_element_type=F32)


def _rope(x, cos, sin_up, sin_dn):
    return (x * cos + pltpu.roll(x, ROT_HALF, 1) * sin_up
            + pltpu.roll(x, LANES - ROT_HALF, 1) * sin_dn)


def _in_proj_kernel(x_ref, g_ref, w_ref, cos_ref, sup_ref, sdn_ref, glu_ref, q_ref, k_ref, v_ref):
    xn = _rms(x_ref[...], g_ref[...]).astype(BF16)
    a1 = _dot(xn, w_ref[:, 0:D_CONV])
    a2 = _dot(xn, w_ref[:, D_CONV:2 * D_CONV])
    glu_ref[...] = a1 / (1.0 + jnp.exp(-a2))
    cos, sup, sdn = cos_ref[...], sup_ref[...], sdn_ref[...]
    c0 = 2 * D_CONV
    for c in range(D_ATTN // 256):
        qc = _dot(xn, w_ref[:, c0 + 256 * c:c0 + 256 * (c + 1)])
        q_ref[:, 256 * c:256 * c + 128] = _rope(qc[:, 0:128], cos, sup, sdn)
        q_ref[:, 256 * c + 128:256 * (c + 1)] = _rope(qc[:, 128:256], cos, sup, sdn)
    c1 = c0 + D_ATTN
    kc = _dot(xn, w_ref[:, c1:c1 + KV_DIM])
    k_ref[:, 0:128] = _rope(kc[:, 0:128], cos, sup, sdn)
    k_ref[:, 128:256] = _rope(kc[:, 128:256], cos, sup, sdn)
    v_ref[...] = _dot(xn, w_ref[:, c1 + KV_DIM:c1 + 2 * KV_DIM])


def _rope_tables(t, offset):
    pos = (offset + jnp.arange(t, dtype=jnp.int32)).astype(F32)
    inv_freq = jnp.power(jnp.float32(ROPE_THETA), -(2.0 * jnp.arange(ROT_HALF, dtype=F32)) / ROT_DIM)
    ang = pos[:, None] * inv_freq[None, :]
    d = jnp.arange(LANES) % HEAD_DIM
    ang_l = ang[:, d % ROT_HALF]
    cos = jnp.where(d[None, :] < ROT_DIM, jnp.cos(ang_l), 1.0)
    sin = jnp.sin(ang_l)
    sin_up = jnp.where((d[None, :] >= ROT_HALF) & (d[None, :] < ROT_DIM), sin, 0.0)
    sin_dn = jnp.where(d[None, :] < ROT_HALF, -sin, 0.0)
    return cos.astype(F32), sin_up.astype(F32), sin_dn.astype(F32)


def _in_proj(x, g, w_bf, pos_offset, tb):
    b, t, d = x.shape
    cos, sup, sdn = _rope_tables(t, pos_offset)
    row = lambda c: pl.BlockSpec((None, tb, c), lambda i, j: (i, j, 0))
    tab = pl.BlockSpec((tb, LANES), lambda i, j: (j, 0))
    return pl.pallas_call(
        _in_proj_kernel,
        grid=(b, t // tb),
        in_specs=[row(d), _resident((1, d)), _resident((d, IN_COLS)), tab, tab, tab],
        out_specs=[row(D_CONV), row(D_ATTN), row(KV_DIM), row(KV_DIM)],
        out_shape=[jax.ShapeDtypeStruct((b, t, c), F32) for c in (D_CONV, D_ATTN, KV_DIM, KV_DIM)],
        compiler_params=_params("parallel", "arbitrary"),
    )(x, g.reshape(1, d), w_bf, cos, sup, sdn)


CONV_ROWS = 128


def _conv_kernel(glu_ref, hist_ref, w_ref, b_ref, lg_ref, lb_ref, o_ref, xin_ref, y_ref, win_ref, *, tb):
    @pl.when(pl.program_id(1) == 0)
    def _():
        xin_ref[0:HIST_PAD, :] = hist_ref[...]

    xin_ref[HIST_PAD:HIST_PAD + tb, :] = glu_ref[...]
    lead = HIST_PAD - HIST
    rc = min(tb, CONV_ROWS)
    for c in range(D_CONV // LANES):
        cs = slice(c * LANES, (c + 1) * LANES)
        for r0 in range(0, tb, rc):
            acc = jnp.zeros((rc, LANES), F32)
            for phase in range(SUBLANES):
                offs = [o for o in range(lead, lead + CONV_WIDTH) if o % SUBLANES == phase]
                span = max(offs) // SUBLANES * SUBLANES
                win_ref[0:rc + span, :] = xin_ref[r0 + phase:r0 + phase + rc + span, cs]
                for o in offs:
                    q = o // SUBLANES * SUBLANES
                    acc = acc + win_ref[q:q + rc, :] * w_ref[o - lead:o - lead + 1, cs]
            y_ref[r0:r0 + rc, cs] = acc + b_ref[:, cs]
    y = y_ref[...]
    mu = jnp.mean(y, axis=-1, keepdims=True)
    yc = y - mu
    var = jnp.mean(yc * yc, axis=-1, keepdims=True)
    z = yc * lax.rsqrt(var + EPS) * lg_ref[...] + lb_ref[...]
    o_ref[...] = z / (1.0 + jnp.exp(-z))
    xin_ref[0:HIST_PAD, :] = xin_ref[tb:tb + HIST_PAD, :]


def _conv(glu, hist, w, bias, ln_g, ln_b, tb):
    b, t, c = glu.shape
    hist_p = jnp.pad(hist.astype(F32), ((0, 0), (HIST_PAD - HIST, 0), (0, 0)))
    w_p = jnp.pad(w, ((0, HIST_PAD - CONV_WIDTH), (0, 0)))
    row = pl.BlockSpec((None, tb, c), lambda i, j: (i, j, 0))
    return pl.pallas_call(
        functools.partial(_conv_kernel, tb=tb),
        grid=(b, t // tb),
        in_specs=[row, pl.BlockSpec((None, HIST_PAD, c), lambda i, j: (i, 0, 0)),
                  _resident((HIST_PAD, c)), _resident((1, c)), _resident((1, c)), _resident((1, c))],
        out_specs=row,
        out_shape=jax.ShapeDtypeStruct((b, t, c), F32),
        scratch_shapes=[pltpu.VMEM((tb + HIST_PAD, c), F32), pltpu.VMEM((tb, c), F32),
                        pltpu.VMEM((min(tb, CONV_ROWS) + HIST_PAD, LANES), F32)],
        compiler_params=_params("parallel", "arbitrary"),
    )(glu, hist_p, w_p, bias.reshape(1, c), ln_g.reshape(1, c), ln_b.reshape(1, c))


Q_BLOCK = 2 * CHUNK


def _attn_banded_kernel(sinks_ref, q_ref, *refs):
    n_kb = Q_BLOCK // CHUNK + WINDOW_CHUNKS
    k_refs, v_refs, o_ref = refs[:n_kb], refs[n_kb:2 * n_kb], refs[2 * n_kb]
    q = q_ref[...].astype(BF16)
    kk = jnp.concatenate([r[...] for r in k_refs], axis=0).astype(BF16)
    vv = jnp.concatenate([r[...] for r in v_refs], axis=0).astype(BF16)
    tk = n_kb * CHUNK
    key_chunk = lax.broadcasted_iota(jnp.int32, (tk, Q_BLOCK), 0) // CHUNK
    qry_chunk = lax.broadcasted_iota(jnp.int32, (tk, Q_BLOCK), 1) // CHUNK
    back = qry_chunk + WINDOW_CHUNKS - key_chunk
    first = pl.program_id(1) * (Q_BLOCK // CHUNK) - WINDOW_CHUNKS
    valid = (back >= 0) & (back <= WINDOW_CHUNKS) & (first + key_chunk >= 0)
    pad = jnp.zeros((tk, HEAD_DIM), BF16)

    def padded(x, g, side):
        xg = x[:, g * HEAD_DIM:(g + 1) * HEAD_DIM]
        return jnp.concatenate([xg, pad] if side == 0 else [pad, xg], axis=1)

    heads = [(g, pr, side) for g in range(N_KV_HEADS) for pr in range(GROUP // 2) for side in range(2)]
    cols = lambda g, pr: slice((g * GROUP // 2 + pr) * LANES, (g * GROUP // 2 + pr + 1) * LANES)
    scores = [_dot_nt(padded(kk, g, side), q[:, cols(g, pr)]) for g, pr, side in heads]
    probs = []
    for (g, pr, side), s in zip(heads, scores):
        sink = sinks_ref[g * GROUP + 2 * pr + side]
        s = jnp.where(valid, s * (HEAD_DIM ** -0.5), NEG)
        m = jnp.maximum(jnp.max(s, axis=0, keepdims=True), sink)
        p = jnp.exp(s - m)
        den = jnp.sum(p, axis=0, keepdims=True) + jnp.exp(sink - m)
        probs.append((p / den).astype(BF16))
    for g in range(N_KV_HEADS):
        for pr in range(GROUP // 2):
            i = heads.index((g, pr, 0))
            o_ref[:, cols(g, pr)] = (_dot_tn(probs[i], padded(vv, g, 0))
                                     + _dot_tn(probs[i + 1], padded(vv, g, 1)))


def _attn_prompt(q, k, v, sinks):
    b, t, _ = q.shape
    per = Q_BLOCK // CHUNK
    n_kb = per + WINDOW_CHUNKS
    qspec = pl.BlockSpec((None, Q_BLOCK, D_ATTN), lambda i, c: (i, c, 0))

    def kspec(j):
        return pl.BlockSpec((None, CHUNK, KV_DIM),
                            lambda i, c: (i, jnp.maximum(c * per + j - WINDOW_CHUNKS, 0), 0))

    kspecs = [kspec(j) for j in range(n_kb)]
    return pl.pallas_call(
        _attn_banded_kernel,
        grid=(b, t // Q_BLOCK),
        in_specs=[pl.BlockSpec(memory_space=pltpu.SMEM), qspec] + kspecs + kspecs,
        out_specs=qspec,
        out_shape=jax.ShapeDtypeStruct((b, t, D_ATTN), F32),
        compiler_params=_params("parallel", "arbitrary"),
    )(sinks, q, *([k] * n_kb), *([v] * n_kb))


def _attn_cached_kernel(sinks_ref, q_ref, kh_ref, kn_ref, vh_ref, vn_ref, o_ref):
    q = q_ref[...]
    tq = q.shape[0]
    kk = jnp.concatenate([kh_ref[...], kn_ref[...]], axis=0).astype(BF16)
    vv = jnp.concatenate([vh_ref[...], vn_ref[...]], axis=0).astype(BF16)
    outs = []
    for g in range(N_KV_HEADS):
        heads = range(g * GROUP, (g + 1) * GROUP)
        qg = jnp.concatenate([q[:, h * HEAD_DIM:(h + 1) * HEAD_DIM] for h in heads], axis=0)
        kg = kk[:, g * HEAD_DIM:(g + 1) * HEAD_DIM]
        vg = vv[:, g * HEAD_DIM:(g + 1) * HEAD_DIM]
        s = _dot_nt(qg.astype(BF16), kg) * (HEAD_DIM ** -0.5)
        sink = jnp.concatenate([jnp.full((tq, 1), sinks_ref[h], F32) for h in heads], axis=0)
        m = jnp.maximum(jnp.max(s, axis=-1, keepdims=True), sink)
        p = jnp.exp(s - m)
        den = jnp.sum(p, axis=-1, keepdims=True) + jnp.exp(sink - m)
        og = _dot((p / den).astype(BF16), vg)
        outs += [og[i * tq:(i + 1) * tq, :] for i in range(GROUP)]
    o_ref[...] = jnp.concatenate(outs, axis=1)


def _attn_cached(q, k_hist, v_hist, k, v, sinks):
    b, t, _ = q.shape
    th = k_hist.shape[1]
    qspec = pl.BlockSpec((None, t, D_ATTN), lambda i: (i, 0, 0))
    hspec = pl.BlockSpec((None, th, KV_DIM), lambda i: (i, 0, 0))
    nspec = pl.BlockSpec((None, t, KV_DIM), lambda i: (i, 0, 0))
    return pl.pallas_call(
        _attn_cached_kernel,
        grid=(b,),
        in_specs=[pl.BlockSpec(memory_space=pltpu.SMEM), qspec, hspec, nspec, hspec, nspec],
        out_specs=qspec,
        out_shape=jax.ShapeDtypeStruct((b, t, D_ATTN), F32),
        compiler_params=_params("parallel"),
    )(sinks, q, k_hist, k, v_hist, v)


def _mem_kv_kernel(m_ref, g_ref, wk_ref, wv_ref, k_ref, v_ref):
    mn = _rms(m_ref[...], g_ref[...]).astype(BF16)
    k_ref[...] = _dot(mn, wk_ref[...])
    v_ref[...] = _dot(mn, wv_ref[...])


def _mem_kv(mem, g, wk_bf, wv_bf):
    b, n, d = mem.shape
    out = pl.BlockSpec((None, n, D_MEM), lambda i: (i, 0, 0))
    return pl.pallas_call(
        _mem_kv_kernel,
        grid=(b,),
        in_specs=[pl.BlockSpec((None, n, d), lambda i: (i, 0, 0)), _resident((1, d)),
                  _resident((d, D_MEM)), _resident((d, D_MEM))],
        out_specs=[out, out],
        out_shape=[jax.ShapeDtypeStruct((b, n, D_MEM), F32)] * 2,
        compiler_params=_params("parallel"),
    )(mem, g.reshape(1, d), wk_bf, wv_bf)


def _mid_kernel(x_ref, conv_ref, attn_ref, wout_ref, gc_ref, wcq_ref, mk_ref, mv_ref, wco_ref,
                gf_ref, wq_ref, h2_ref, xn_ref, qp_ref):
    h1 = (x_ref[...] + _dot(conv_ref[...].astype(BF16), wout_ref[0:D_CONV, :])
          + _dot(attn_ref[...].astype(BF16), wout_ref[D_CONV:D_CONV + D_ATTN, :]))
    qc = _dot(_rms(h1, gc_ref[...]).astype(BF16), wcq_ref[...])
    mk = mk_ref[...].astype(BF16)
    mv = mv_ref[...].astype(BF16)
    outs = []
    for h in range(MEM_HEADS):
        hs = slice(h * MEM_HEAD_DIM, (h + 1) * MEM_HEAD_DIM)
        s = _dot_nt(qc[:, hs].astype(BF16), mk[:, hs]) * (MEM_HEAD_DIM ** -0.5)
        p = jnp.exp(s - jnp.max(s, axis=-1, keepdims=True))
        p = p / jnp.sum(p, axis=-1, keepdims=True)
        outs.append(_dot(p.astype(BF16), mv[:, hs]))
    o = jnp.concatenate(outs, axis=1).astype(BF16)
    h2 = h1 + _dot(o, wco_ref[...])
    h2_ref[...] = h2
    xn = _rms(h2, gf_ref[...]).astype(BF16)
    xn_ref[...] = xn
    qp_ref[...] = _dot(xn, wq_ref[...])


def _mid(x, conv_out, attn_out, mk, mv, wout_bf, gc, wcq_bf, wco_bf, gf, wq_bf, tb):
    b, t, d = x.shape
    row = lambda c: pl.BlockSpec((None, tb, c), lambda i, j: (i, j, 0))
    mem = pl.BlockSpec((None, N_MEM, D_MEM), lambda i, j: (i, 0, 0))
    dq = wq_bf.shape[1]
    return pl.pallas_call(
        _mid_kernel,
        grid=(b, t // tb),
        in_specs=[row(d), row(D_CONV), row(D_ATTN), _resident((D_CONV + D_ATTN, d)), _resident((1, d)),
                  _resident((d, D_MEM)), mem, mem, _resident((D_MEM, d)), _resident((1, d)),
                  _resident((d, dq))],
        out_specs=[row(d), row(d), row(dq)],
        out_shape=[jax.ShapeDtypeStruct((b, t, d), F32), jax.ShapeDtypeStruct((b, t, d), BF16),
                   jax.ShapeDtypeStruct((b, t, dq), F32)],
        compiler_params=_params("parallel", "arbitrary"),
    )(x, conv_out, attn_out, wout_bf, gc.reshape(1, d), wcq_bf, mk, mv, wco_bf, gf.reshape(1, d), wq_bf)


CAND_WIDE = PEER_TOPK // 2
CAND_ROWS = PEER_TOPK + (CAND_WIDE - 1) * CAND_WIDE + CAND_WIDE


def _topk_rank(jobs):
    for w_ref, r_ref, _, _ in jobs:
        r_ref[...] = jnp.full(w_ref.shape, float(PEER_TOPK), F32)

    def body(k, carry):
        for w_ref, r_ref, sv_ref, rows_ref in jobs:
            w = w_ref[...]
            rows = rows_ref[...]
            m = jnp.max(w, axis=0, keepdims=True)
            first = jnp.min(jnp.where(w == m, rows, float(w.shape[0])), axis=0, keepdims=True)
            hit = rows == first
            sv_ref[pl.ds(k, 1), :] = m
            r_ref[...] = jnp.where(hit, lax.convert_element_type(k, F32), r_ref[...])
            w_ref[...] = jnp.where(hit, -jnp.inf, w)
        return carry

    lax.fori_loop(0, PEER_TOPK, body, 0)


ROUTER_HEADS = 2


def _bf16_twice(x):
    hi = pltpu.bitcast(x.astype(BF16).astype(F32), jnp.uint32)
    return hi | (hi >> 16)


def _router_kernel(q_ref, keys_ref, lf_ref, e1_ref, r2_ref, e2_ref,
                   s_ref, w_ref, rk_ref, sv_ref, cand_ref, rc_ref, svc_ref, rows_ref, crow_ref):
    half = PEER_DK // 2
    tr = q_ref.shape[0]
    rows_ref[...] = lax.broadcasted_iota(jnp.int32, (N_KEYS, tr), 0).astype(F32)
    crow_ref[...] = lax.broadcasted_iota(jnp.int32, (CAND_ROWS, tr), 0).astype(F32)
    for j in range(2 * ROUTER_HEADS):
        s = _dot_nt(keys_ref[j // 2, j % 2], q_ref[:, j * half:(j + 1) * half])
        s_ref[j] = s
        w_ref[j] = s
    _topk_rank([(w_ref.at[j], rk_ref.at[j], sv_ref.at[j], rows_ref) for j in range(2 * ROUTER_HEADS)])
    tail = CAND_ROWS - CAND_WIDE
    for g in range(ROUTER_HEADS):
        sv1, sv2, cand = sv_ref.at[2 * g], sv_ref.at[2 * g + 1], cand_ref.at[g]
        cand[0:PEER_TOPK, :] = sv1[0:1, :] + sv2[...]
        for k1 in range(1, CAND_WIDE):
            r0 = CAND_WIDE * (k1 + 1)
            cand[r0:r0 + CAND_WIDE, :] = sv1[k1:k1 + 1, :] + sv2[0:CAND_WIDE, :]
        cand[tail:CAND_ROWS, :] = sv1[CAND_WIDE:PEER_TOPK, :] + sv2[0:1, :]
    _topk_rank([(cand_ref.at[g], rc_ref.at[g], svc_ref.at[g], crow_ref) for g in range(ROUTER_HEADS)])
    for g in range(ROUTER_HEADS):
        sv1, sv2 = sv_ref[2 * g], sv_ref[2 * g + 1]
        g1 = jnp.exp(sv1 - sv1[0:1, :])
        g2 = jnp.exp(sv2 - sv2[0:1, :])
        sel = jnp.where(rc_ref[g] < float(PEER_TOPK), 1.0, 0.0)
        r1 = rk_ref[2 * g]
        blk = sel[0:PEER_TOPK, :]
        z = jnp.sum(blk * g2, axis=0, keepdims=True)
        lf = jnp.where(r1 == 0.0, jnp.sum(blk, axis=0, keepdims=True), 0.0)
        for k1 in range(1, CAND_WIDE):
            r0 = CAND_WIDE * (k1 + 1)
            blk = sel[r0:r0 + CAND_WIDE, :]
            z = z + g1[k1:k1 + 1, :] * jnp.sum(blk * g2[0:CAND_WIDE, :], axis=0, keepdims=True)
            lf = lf + jnp.where(r1 == float(k1), jnp.sum(blk, axis=0, keepdims=True), 0.0)
        blk = sel[tail:CAND_ROWS, :]
        z = z + jnp.sum(blk * g1[CAND_WIDE:PEER_TOPK, :], axis=0, keepdims=True)
        for r in range(CAND_WIDE):
            lf = lf + jnp.where(r1 == float(CAND_WIDE + r), blk[r:r + 1, :], 0.0)
        lf_ref[g] = _bf16_twice(lf)
        e1_ref[g] = _bf16_twice(jnp.exp(s_ref[2 * g] - sv1[0:1, :]) / z)
        r2_ref[g] = pltpu.bitcast(rk_ref[2 * g + 1].astype(BF16), jnp.uint32)
        e2_ref[g] = pltpu.bitcast(jnp.exp(s_ref[2 * g + 1] - sv2[0:1, :]).astype(BF16), jnp.uint32)


def _router(qp, keys, tr):
    n = qp.shape[0]
    tab = pl.BlockSpec((ROUTER_HEADS, N_KEYS, tr), lambda t, h: (h, 0, t))
    ptab = pl.BlockSpec((ROUTER_HEADS, N_KEYS // 2, tr), lambda t, h: (h, 0, t))
    twice_tab = jax.ShapeDtypeStruct((PEER_HEADS, N_KEYS, n), jnp.uint32)
    packed_tab = jax.ShapeDtypeStruct((PEER_HEADS, N_KEYS // 2, n), jnp.uint32)
    vm = lambda lead, rows: pltpu.VMEM((lead, rows, tr), F32)
    return pl.pallas_call(
        _router_kernel,
        grid=(n // tr, PEER_HEADS // ROUTER_HEADS),
        in_specs=[pl.BlockSpec((tr, ROUTER_HEADS * PEER_DK), lambda t, h: (t, h)),
                  pl.BlockSpec((ROUTER_HEADS, 2, N_KEYS, PEER_DK // 2), lambda t, h: (h, 0, 0, 0))],
        out_specs=[tab, tab, ptab, ptab],
        out_shape=[twice_tab, twice_tab, packed_tab, packed_tab],
        scratch_shapes=[vm(2 * ROUTER_HEADS, N_KEYS), vm(2 * ROUTER_HEADS, N_KEYS), vm(2 * ROUTER_HEADS, N_KEYS),
                        vm(2 * ROUTER_HEADS, PEER_TOPK), vm(ROUTER_HEADS, CAND_ROWS), vm(ROUTER_HEADS, CAND_ROWS),
                        vm(ROUTER_HEADS, PEER_TOPK), pltpu.VMEM((N_KEYS, tr), F32),
                        pltpu.VMEM((CAND_ROWS, tr), F32)],
        compiler_params=_params("parallel", "arbitrary"),
    )(qp, keys)


def _gelu(x):
    return 0.5 * x * (1.0 + lax.erf(x * (2.0 ** -0.5)))


PACK = 16


def _experts_kernel(x_ref, h_ref, lf_ref, e1_ref, r2_ref, e2_ref, u_ref, v_ref, gfin_ref, o_ref, s_ref, a_ref,
                    lrow_ref, erow_ref, *, et):
    e = pl.program_id(1)
    tp = x_ref.shape[0]

    @pl.when(e == 0)
    def _():
        o_ref[...] = h_ref[...]

    s_ref[...] = _dot_nt(u_ref[...], x_ref[...])
    zero = jnp.zeros((PACK, LANES), BF16)
    n_sub = et // N_KEYS
    i0 = pl.multiple_of(e * n_sub, n_sub)
    for h in range(PEER_HEADS):
        lrow_ref[h] = lf_ref[h, pl.ds(i0, n_sub), :]
        erow_ref[h] = e1_ref[h, pl.ds(i0, n_sub), :]

    def row(ref, h, sub, ls):
        words = jnp.broadcast_to(ref[h, sub:sub + 1, ls], (SUBLANES, LANES))
        return pltpu.bitcast(words, BF16)

    for tb in range(tp // LANES):
        ls = slice(tb * LANES, (tb + 1) * LANES)
        for sub in range(n_sub):
            lrows = [row(lrow_ref, h, sub, ls) for h in range(PEER_HEADS)]
            erows = [row(erow_ref, h, sub, ls) for h in range(PEER_HEADS)]
            for c in range(N_KEYS // PACK):
                js = slice(c * SUBLANES, (c + 1) * SUBLANES)
                rs = slice(sub * N_KEYS + c * PACK, sub * N_KEYS + (c + 1) * PACK)
                w = None
                for h in range(PEER_HEADS):
                    r2 = pltpu.bitcast(r2_ref[h, js, ls], BF16)
                    e2 = pltpu.bitcast(e2_ref[h, js, ls], BF16)
                    t = jnp.where(r2 < lrows[h], e2, zero) * erows[h]
                    w = t if w is None else w + t
                a_ref[rs, ls] = w * _gelu(s_ref[rs, ls]).astype(BF16)
    o_ref[...] += _dot_tn(a_ref[...], v_ref[...])

    @pl.when(e == pl.num_programs(1) - 1)
    def _():
        o_ref[...] = _rms(o_ref[...], gfin_ref[...])


def _experts(xn, h2, tabs, u_bf, v_bf, g_final, tp, et):
    n, d = xn.shape
    assert et % (SUBLANES * N_KEYS) == 0, "a step's first-half keys must be whole sublane tiles of the gate tables"
    tab = pl.BlockSpec((PEER_HEADS, N_KEYS, tp), lambda t, e: (0, 0, t))
    ptab = pl.BlockSpec((PEER_HEADS, N_KEYS // 2, tp), lambda t, e: (0, 0, t))
    row = pl.BlockSpec((tp, d), lambda t, e: (t, 0))
    wt = pl.BlockSpec((et, d), lambda t, e: (e, 0))
    return pl.pallas_call(
        functools.partial(_experts_kernel, et=et),
        grid=(n // tp, N_EXPERTS // et),
        in_specs=[row, row, tab, tab, ptab, ptab, wt, wt, _resident((1, d))],
        out_specs=row,
        out_shape=jax.ShapeDtypeStruct((n, d), F32),
        scratch_shapes=[pltpu.VMEM((et, tp), F32), pltpu.VMEM((et, tp), BF16),
                        pltpu.VMEM((PEER_HEADS, et // N_KEYS, tp), jnp.uint32),
                        pltpu.VMEM((PEER_HEADS, et // N_KEYS, tp), jnp.uint32)],
        compiler_params=_params("parallel", "arbitrary"),
    )(xn, h2, *tabs, u_bf, v_bf, g_final.reshape(1, d))


def _layer(x, pos_offset, conv_hist, k_hist, v_hist, mk, mv, wts, g_final, tb, tp):
    (g_mix, w_in, dw_w, dw_b, ln_g, ln_b, sinks, w_out, g_cross, w_cq, w_co, g_ffn, wq, keys, u, v) = wts
    b, t, d = x.shape
    glu, q, k, vv = _in_proj(x, g_mix, w_in, pos_offset, tb)
    if conv_hist is None:
        conv_hist = jnp.zeros((b, HIST, D_CONV), F32)
    conv_out = _conv(glu, conv_hist, dw_w, dw_b, ln_g, ln_b, tb)
    if k_hist is None:
        attn = _attn_prompt(q, k, vv, sinks)
        new_k, new_v = k[:, -WINDOW:], vv[:, -WINDOW:]
    else:
        attn = _attn_cached(q, k_hist, v_hist, k, vv, sinks)
        new_k = jnp.concatenate([k_hist, k], axis=1)[:, -WINDOW:]
        new_v = jnp.concatenate([v_hist, vv], axis=1)[:, -WINDOW:]
    h2, xn, qp = _mid(x, conv_out, attn, mk, mv, w_out, g_cross, w_cq, w_co, g_ffn, wq, tb)
    n = b * t
    tabs = _router(qp.reshape(n, -1), keys, min(256, n))
    y = _experts(xn.reshape(n, d), h2.reshape(n, d), tabs, u, v, g_final, tp, 1024)
    new_conv = glu[:, -HIST:]
    return (y.reshape(b, t, d), new_conv,
            new_k.reshape(b, WINDOW, N_KV_HEADS, HEAD_DIM), new_v.reshape(b, WINDOW, N_KV_HEADS, HEAD_DIM))


def kernel(x_prompt, x_sample, mem_prompt, cache_conv, cache_swa_k, cache_swa_v, cache_mem_k, cache_mem_v, norm_mix_g, w_in, conv_dw_w, conv_dw_b, conv_ln_g, conv_ln_b, swa_sinks, w_out, norm_cross_g, norm_mem_g, w_cq, w_ck, w_cv, w_co, norm_ffn_g, peer_wq, peer_subkeys, peer_u, peer_v, norm_final_g):
    depth = w_in.shape[0]
    assert depth == 1, "the final rmsnorm is fused into the last layer's expert kernel"
    bs = x_sample.shape[0]
    l = 0
    bf = lambda a: a.astype(BF16)
    wts = (norm_mix_g[l], bf(w_in[l]), conv_dw_w[l], conv_dw_b[l], conv_ln_g[l], conv_ln_b[l], swa_sinks[l],
           bf(w_out[l]), norm_cross_g[l], bf(w_cq[l]), bf(w_co[l]), norm_ffn_g[l], bf(peer_wq[l]),
           peer_subkeys[l], bf(peer_u[l]), bf(peer_v[l]))
    mk, mv = _mem_kv(mem_prompt, norm_mem_g[l], bf(w_ck[l]), bf(w_cv[l]))
    yp, conv_p, k_p, v_p = _layer(x_prompt, 0, None, None, None, mk, mv, wts, norm_final_g,
                                  tb=256, tp=512)
    ys, conv_s, k_s, v_s = _layer(
        x_sample, PAST_LEN, cache_conv[l], cache_swa_k[l].reshape(bs, WINDOW, KV_DIM),
        cache_swa_v[l].reshape(bs, WINDOW, KV_DIM), cache_mem_k[l].reshape(bs, N_MEM, D_MEM),
        cache_mem_v[l].reshape(bs, N_MEM, D_MEM), wts, norm_final_g, tb=x_sample.shape[1], tp=256)
    bp = x_prompt.shape[0]
    mem_shape = (1, bp, N_MEM, MEM_HEADS, MEM_HEAD_DIM)
    return (yp, ys, conv_p[None], k_p[None], v_p[None], mk.reshape(mem_shape), mv.reshape(mem_shape),
            conv_s[None], k_s[None], v_s[None])
```

```python
import functools

import jax
import jax.numpy as jnp
from jax import lax
from jax.experimental import pallas as pl
from jax.experimental.pallas import tpu as pltpu

F32 = jnp.float32
BF16 = jnp.bfloat16

D_MODEL = 2048
PAST_LEN = 1024
CHUNK = 64
D_CONV = 1024
CONV_WIDTH = 31
HIST = CONV_WIDTH - 1
HIST_PAD = 32
N_HEADS = 16
N_KV_HEADS = 4
HEAD_DIM = 64
GROUP = N_HEADS // N_KV_HEADS
D_ATTN = N_HEADS * HEAD_DIM
KV_DIM = N_KV_HEADS * HEAD_DIM
IN_COLS = 2 * D_CONV + D_ATTN + 2 * KV_DIM
ROT_DIM = HEAD_DIM // 4
ROT_HALF = ROT_DIM // 2
ROPE_THETA = 500000.0
WINDOW = 128
WINDOW_CHUNKS = WINDOW // CHUNK
N_MEM = 256
MEM_HEADS = 4
MEM_HEAD_DIM = 128
D_MEM = MEM_HEADS * MEM_HEAD_DIM
PEER_HEADS = 8
N_KEYS = 128
N_EXPERTS = N_KEYS * N_KEYS
PEER_DK = 256
PEER_TOPK = 16
EPS = 1e-6
NEG = -1e30

LANES = 128
SUBLANES = 8
VMEM_LIMIT = 56 * 1024 * 1024


def _params(*sem):
    return pltpu.CompilerParams(dimension_semantics=sem, vmem_limit_bytes=VMEM_LIMIT)


def _resident(shape):
    return pl.BlockSpec(shape, lambda *_: (0,) * len(shape), pipeline_mode=pl.Buffered(1))


def _rms(x, g):
    return x * lax.rsqrt(jnp.mean(x * x, axis=-1, keepdims=True) + EPS) * g


def _dot(a, b):
    return jnp.dot(a, b, preferred_element_type=F32)


def _dot_nt(a, b):
    return lax.dot_general(a, b, (((1,), (1,)), ((), ())), preferred_element_type=F32)


def _dot_tn(a, b):
    return lax.dot_general(a, b, (((0,), (0,)), ((), ())), preferred_element_type=F32)


def _rope(x, cos, sin_up, sin_dn):
    return (x * cos + pltpu.roll(x, ROT_HALF, 1) * sin_up
            + pltpu.roll(x, LANES - ROT_HALF, 1) * sin_dn)


def _in_proj_kernel(x_ref, g_ref, w_ref, cos_ref, sup_ref, sdn_ref, glu_ref, q_ref, k_ref, v_ref):
    xn = _rms(x_ref[...], g_ref[...]).astype(BF16)
    a1 = _dot(xn, w_ref[:, 0:D_CONV])
    a2 = _dot(xn, w_ref[:, D_CONV:2 * D_CONV])
    glu_ref[...] = a1 / (1.0 + jnp.exp(-a2))
    cos, sup, sdn = cos_ref[...], sup_ref[...], sdn_ref[...]
    c0 = 2 * D_CONV
    for c in range(D_ATTN // 256):
        qc = _dot(xn, w_ref[:, c0 + 256 * c:c0 + 256 * (c + 1)])
        q_ref[:, 256 * c:256 * c + 128] = _rope(qc[:, 0:128], cos, sup, sdn)
        q_ref[:, 256 * c + 128:256 * (c + 1)] = _rope(qc[:, 128:256], cos, sup, sdn)
    c1 = c0 + D_ATTN
    kc = _dot(xn, w_ref[:, c1:c1 + KV_DIM])
    k_ref[:, 0:128] = _rope(kc[:, 0:128], cos, sup, sdn)
    k_ref[:, 128:256] = _rope(kc[:, 128:256], cos, sup, sdn)
    v_ref[...] = _dot(xn, w_ref[:, c1 + KV_DIM:c1 + 2 * KV_DIM])


def _rope_tables(t, offset):
    pos = (offset + jnp.arange(t, dtype=jnp.int32)).astype(F32)
    inv_freq = jnp.power(jnp.float32(ROPE_THETA), -(2.0 * jnp.arange(ROT_HALF, dtype=F32)) / ROT_DIM)
    ang = pos[:, None] * inv_freq[None, :]
    d = jnp.arange(LANES) % HEAD_DIM
    ang_l = ang[:, d % ROT_HALF]
    cos = jnp.where(d[None, :] < ROT_DIM, jnp.cos(ang_l), 1.0)
    sin = jnp.sin(ang_l)
    sin_up = jnp.where((d[None, :] >= ROT_HALF) & (d[None, :] < ROT_DIM), sin, 0.0)
    sin_dn = jnp.where(d[None, :] < ROT_HALF, -sin, 0.0)
    return cos.astype(F32), sin_up.astype(F32), sin_dn.astype(F32)


def _in_proj(x, g, w_bf, pos_offset, tb):
    b, t, d = x.shape
    cos, sup, sdn = _rope_tables(t, pos_offset)
    row = lambda c: pl.BlockSpec((None, tb, c), lambda i, j: (i, j, 0))
    tab = pl.BlockSpec((tb, LANES), lambda i, j: (j, 0))
    return pl.pallas_call(
        _in_proj_kernel,
        grid=(b, t // tb),
        in_specs=[row(d), _resident((1, d)), _resident((d, IN_COLS)), tab, tab, tab],
        out_specs=[row(D_CONV), row(D_ATTN), row(KV_DIM), row(KV_DIM)],
        out_shape=[jax.ShapeDtypeStruct((b, t, c), F32) for c in (D_CONV, D_ATTN, KV_DIM, KV_DIM)],
        compiler_params=_params("parallel", "arbitrary"),
    )(x, g.reshape(1, d), w_bf, cos, sup, sdn)


CONV_ROWS = 128


def _conv_kernel(glu_ref, hist_ref, w_ref, b_ref, lg_ref, lb_ref, o_ref, xin_ref, y_ref, win_ref, *, tb):
    @pl.when(pl.program_id(1) == 0)
    def _():
        xin_ref[0:HIST_PAD, :] = hist_ref[...]

    xin_ref[HIST_PAD:HIST_PAD + tb, :] = glu_ref[...]
    lead = HIST_PAD - HIST
    rc = min(tb, CONV_ROWS)
    for c in range(D_CONV // LANES):
        cs = slice(c * LANES, (c + 1) * LANES)
        for r0 in range(0, tb, rc):
            acc = jnp.zeros((rc, LANES), F32)
            for phase in range(SUBLANES):
                offs = [o for o in range(lead, lead + CONV_WIDTH) if o % SUBLANES == phase]
                span = max(offs) // SUBLANES * SUBLANES
                win_ref[0:rc + span, :] = xin_ref[r0 + phase:r0 + phase + rc + span, cs]
                for o in offs:
                    q = o // SUBLANES * SUBLANES
                    acc = acc + win_ref[q:q + rc, :] * w_ref[o - lead:o - lead + 1, cs]
            y_ref[r0:r0 + rc, cs] = acc + b_ref[:, cs]
    y = y_ref[...]
    mu = jnp.mean(y, axis=-1, keepdims=True)
    yc = y - mu
    var = jnp.mean(yc * yc, axis=-1, keepdims=True)
    z = yc * lax.rsqrt(var + EPS) * lg_ref[...] + lb_ref[...]
    o_ref[...] = z / (1.0 + jnp.exp(-z))
    xin_ref[0:HIST_PAD, :] = xin_ref[tb:tb + HIST_PAD, :]


def _conv(glu, hist, w, bias, ln_g, ln_b, tb):
    b, t, c = glu.shape
    hist_p = jnp.pad(hist.astype(F32), ((0, 0), (HIST_PAD - HIST, 0), (0, 0)))
    w_p = jnp.pad(w, ((0, HIST_PAD - CONV_WIDTH), (0, 0)))
    row = pl.BlockSpec((None, tb, c), lambda i, j: (i, j, 0))
    return pl.pallas_call(
        functools.partial(_conv_kernel, tb=tb),
        grid=(b, t // tb),
        in_specs=[row, pl.BlockSpec((None, HIST_PAD, c), lambda i, j: (i, 0, 0)),
                  _resident((HIST_PAD, c)), _resident((1, c)), _resident((1, c)), _resident((1, c))],
        out_specs=row,
        out_shape=jax.ShapeDtypeStruct((b, t, c), F32),
        scratch_shapes=[pltpu.VMEM((tb + HIST_PAD, c), F32), pltpu.VMEM((tb, c), F32),
                        pltpu.VMEM((min(tb, CONV_ROWS) + HIST_PAD, LANES), F32)],
        compiler_params=_params("parallel", "arbitrary"),
    )(glu, hist_p, w_p, bias.reshape(1, c), ln_g.reshape(1, c), ln_b.reshape(1, c))


Q_BLOCK = 2 * CHUNK


def _attn_banded_kernel(sinks_ref, q_ref, *refs):
    n_kb = Q_BLOCK // CHUNK + WINDOW_CHUNKS
    k_refs, v_refs, o_ref = refs[:n_kb], refs[n_kb:2 * n_kb], refs[2 * n_kb]
    q = q_ref[...].astype(BF16)
    kk = jnp.concatenate([r[...] for r in k_refs], axis=0).astype(BF16)
    vv = jnp.concatenate([r[...] for r in v_refs], axis=0).astype(BF16)
    tk = n_kb * CHUNK
    key_chunk = lax.broadcasted_iota(jnp.int32, (tk, Q_BLOCK), 0) // CHUNK
    qry_chunk = lax.broadcasted_iota(jnp.int32, (tk, Q_BLOCK), 1) // CHUNK
    back = qry_chunk + WINDOW_CHUNKS - key_chunk
    first = pl.program_id(1) * (Q_BLOCK // CHUNK) - WINDOW_CHUNKS
    valid = (back >= 0) & (back <= WINDOW_CHUNKS) & (first + key_chunk >= 0)
    pad = jnp.zeros((tk, HEAD_DIM), BF16)

    def padded(x, g, side):
        xg = x[:, g * HEAD_DIM:(g + 1) * HEAD_DIM]
        return jnp.concatenate([xg, pad] if side == 0 else [pad, xg], axis=1)

    heads = [(g, pr, side) for g in range(N_KV_HEADS) for pr in range(GROUP // 2) for side in range(2)]
    cols = lambda g, pr: slice((g * GROUP // 2 + pr) * LANES, (g * GROUP // 2 + pr + 1) * LANES)
    scores = [_dot_nt(padded(kk, g, side), q[:, cols(g, pr)]) for g, pr, side in heads]
    probs = []
    for (g, pr, side), s in zip(heads, scores):
        sink = sinks_ref[g * GROUP + 2 * pr + side]
        s = jnp.where(valid, s * (HEAD_DIM ** -0.5), NEG)
        m = jnp.maximum(jnp.max(s, axis=0, keepdims=True), sink)
        p = jnp.exp(s - m)
        den = jnp.sum(p, axis=0, keepdims=True) + jnp.exp(sink - m)
        probs.append((p / den).astype(BF16))
    for g in range(N_KV_HEADS):
        for pr in range(GROUP // 2):
            i = heads.index((g, pr, 0))
            o_ref[:, cols(g, pr)] = (_dot_tn(probs[i], padded(vv, g, 0))
                                     + _dot_tn(probs[i + 1], padded(vv, g, 1)))


def _attn_prompt(q, k, v, sinks):
    b, t, _ = q.shape
    per = Q_BLOCK // CHUNK
    n_kb = per + WINDOW_CHUNKS
    qspec = pl.BlockSpec((None, Q_BLOCK, D_ATTN), lambda i, c: (i, c, 0))

    def kspec(j):
        return pl.BlockSpec((None, CHUNK, KV_DIM),
                            lambda i, c: (i, jnp.maximum(c * per + j - WINDOW_CHUNKS, 0), 0))

    kspecs = [kspec(j) for j in range(n_kb)]
    return pl.pallas_call(
        _attn_banded_kernel,
        grid=(b, t // Q_BLOCK),
        in_specs=[pl.BlockSpec(memory_space=pltpu.SMEM), qspec] + kspecs + kspecs,
        out_specs=qspec,
        out_shape=jax.ShapeDtypeStruct((b, t, D_ATTN), F32),
        compiler_params=_params("parallel", "arbitrary"),
    )(sinks, q, *([k] * n_kb), *([v] * n_kb))


def _attn_cached_kernel(sinks_ref, q_ref, kh_ref, kn_ref, vh_ref, vn_ref, o_ref):
    q = q_ref[...]
    tq = q.shape[0]
    kk = jnp.concatenate([kh_ref[...], kn_ref[...]], axis=0).astype(BF16)
    vv = jnp.concatenate([vh_ref[...], vn_ref[...]], axis=0).astype(BF16)
    outs = []
    for g in range(N_KV_HEADS):
        heads = range(g * GROUP, (g + 1) * GROUP)
        qg = jnp.concatenate([q[:, h * HEAD_DIM:(h + 1) * HEAD_DIM] for h in heads], axis=0)
        kg = kk[:, g * HEAD_DIM:(g + 1) * HEAD_DIM]
        vg = vv[:, g * HEAD_DIM:(g + 1) * HEAD_DIM]
        s = _dot_nt(qg.astype(BF16), kg) * (HEAD_DIM ** -0.5)
        sink = jnp.concatenate([jnp.full((tq, 1), sinks_ref[h], F32) for h in heads], axis=0)
        m = jnp.maximum(jnp.max(s, axis=-1, keepdims=True), sink)
        p = jnp.exp(s - m)
        den = jnp.sum(p, axis=-1, keepdims=True) + jnp.exp(sink - m)
        og = _dot((p / den).astype(BF16), vg)
        outs += [og[i * tq:(i + 1) * tq, :] for i in range(GROUP)]
    o_ref[...] = jnp.concatenate(outs, axis=1)


def _attn_cached(q, k_hist, v_hist, k, v, sinks):
    b, t, _ = q.shape
    th = k_hist.shape[1]
    qspec = pl.BlockSpec((None, t, D_ATTN), lambda i: (i, 0, 0))
    hspec = pl.BlockSpec((None, th, KV_DIM), lambda i: (i, 0, 0))
    nspec = pl.BlockSpec((None, t, KV_DIM), lambda i: (i, 0, 0))
    return pl.pallas_call(
        _attn_cached_kernel,
        grid=(b,),
        in_specs=[pl.BlockSpec(memory_space=pltpu.SMEM), qspec, hspec, nspec, hspec, nspec],
        out_specs=qspec,
        out_shape=jax.ShapeDtypeStruct((b, t, D_ATTN), F32),
        compiler_params=_params("parallel"),
    )(sinks, q, k_hist, k, v_hist, v)


def _mem_kv_kernel(m_ref, g_ref, wk_ref, wv_ref, k_ref, v_ref):
    mn = _rms(m_ref[...], g_ref[...]).astype(BF16)
    k_ref[...] = _dot(mn, wk_ref[...])
    v_ref[...] = _dot(mn, wv_ref[...])


def _mem_kv(mem, g, wk_bf, wv_bf):
    b, n, d = mem.shape
    out = pl.BlockSpec((None, n, D_MEM), lambda i: (i, 0, 0))
    return pl.pallas_call(
        _mem_kv_kernel,
        grid=(b,),
        in_specs=[pl.BlockSpec((None, n, d), lambda i: (i, 0, 0)), _resident((1, d)),
                  _resident((d, D_MEM)), _resident((d, D_MEM))],
        out_specs=[out, out],
        out_shape=[jax.ShapeDtypeStruct((b, n, D_MEM), F32)] * 2,
        compiler_params=_params("parallel"),
    )(mem, g.reshape(1, d), wk_bf, wv_bf)


def _mid_kernel(x_ref, conv_ref, attn_ref, wout_ref, gc_ref, wcq_ref, mk_ref, mv_ref, wco_ref,
                gf_ref, wq_ref, h2_ref, xn_ref, qp_ref):
    h1 = (x_ref[...] + _dot(conv_ref[...].astype(BF16), wout_ref[0:D_CONV, :])
          + _dot(attn_ref[...].astype(BF16), wout_ref[D_CONV:D_CONV + D_ATTN, :]))
    qc = _dot(_rms(h1, gc_ref[...]).astype(BF16), wcq_ref[...])
    mk = mk_ref[...].astype(BF16)
    mv = mv_ref[...].astype(BF16)
    outs = []
    for h in range(MEM_HEADS):
        hs = slice(h * MEM_HEAD_DIM, (h + 1) * MEM_HEAD_DIM)
        s = _dot_nt(qc[:, hs].astype(BF16), mk[:, hs]) * (MEM_HEAD_DIM ** -0.5)
        p = jnp.exp(s - jnp.max(s, axis=-1, keepdims=True))
        p = p / jnp.sum(p, axis=-1, keepdims=True)
        outs.append(_dot(p.astype(BF16), mv[:, hs]))
    o = jnp.concatenate(outs, axis=1).astype(BF16)
    h2 = h1 + _dot(o, wco_ref[...])
    h2_ref[...] = h2
    xn = _rms(h2, gf_ref[...]).astype(BF16)
    xn_ref[...] = xn
    qp_ref[...] = _dot(xn, wq_ref[...])


def _mid(x, conv_out, attn_out, mk, mv, wout_bf, gc, wcq_bf, wco_bf, gf, wq_bf, tb):
    b, t, d = x.shape
    row = lambda c: pl.BlockSpec((None, tb, c), lambda i, j: (i, j, 0))
    mem = pl.BlockSpec((None, N_MEM, D_MEM), lambda i, j: (i, 0, 0))
    dq = wq_bf.shape[1]
    return pl.pallas_call(
        _mid_kernel,
        grid=(b, t // tb),
        in_specs=[row(d), row(D_CONV), row(D_ATTN), _resident((D_CONV + D_ATTN, d)), _resident((1, d)),
                  _resident((d, D_MEM)), mem, mem, _resident((D_MEM, d)), _resident((1, d)),
                  _resident((d, dq))],
        out_specs=[row(d), row(d), row(dq)],
        out_shape=[jax.ShapeDtypeStruct((b, t, d), F32), jax.ShapeDtypeStruct((b, t, d), BF16),
                   jax.ShapeDtypeStruct((b, t, dq), F32)],
        compiler_params=_params("parallel", "arbitrary"),
    )(x, conv_out, attn_out, wout_bf, gc.reshape(1, d), wcq_bf, mk, mv, wco_bf, gf.reshape(1, d), wq_bf)


CAND_WIDE = PEER_TOPK // 2
CAND_ROWS = PEER_TOPK + (CAND_WIDE - 1) * CAND_WIDE + CAND_WIDE


def _topk_rank(jobs, refill):
    def run(break_ties):
        refill()
        for w_ref, r_ref, _, _ in jobs:
            r_ref[...] = jnp.full(w_ref.shape, float(PEER_TOPK), F32)

        def body(k, carry):
            for w_ref, r_ref, sv_ref, rows_ref in jobs:
                w = w_ref[...]
                m = jnp.max(w, axis=0, keepdims=True)
                hit = w == m
                if break_ties:
                    rows = rows_ref[...]
                    first = jnp.min(jnp.where(hit, rows, float(w.shape[0])), axis=0, keepdims=True)
                    hit = rows == first
                sv_ref[pl.ds(k, 1), :] = m
                r_ref[...] = jnp.where(hit, lax.convert_element_type(k, F32), r_ref[...])
                w_ref[...] = jnp.where(hit, -jnp.inf, w)
            return carry

        lax.fori_loop(0, PEER_TOPK, body, 0)

    run(False)
    most = None
    for _, r_ref, _, _ in jobs:
        taken = jnp.sum(jnp.where(r_ref[...] < float(PEER_TOPK), 1.0, 0.0), axis=0, keepdims=True)
        worst = jnp.max(taken, axis=1, keepdims=True)
        most = worst if most is None else jnp.maximum(most, worst)

    @pl.when(most[0, 0] > float(PEER_TOPK))
    def _():
        run(True)


ROUTER_HEADS = 2


def _bf16_twice(x):
    hi = pltpu.bitcast(x.astype(BF16).astype(F32), jnp.uint32)
    return hi | (hi >> 16)


def _router_kernel(q_ref, keys_ref, lf_ref, e1_ref, r2_ref, e2_ref,
                   s_ref, w_ref, rk_ref, sv_ref, cand_ref, rc_ref, svc_ref, rows_ref, crow_ref):
    half = PEER_DK // 2
    tr = q_ref.shape[0]
    rows_ref[...] = lax.broadcasted_iota(jnp.int32, (N_KEYS, tr), 0).astype(F32)
    crow_ref[...] = lax.broadcasted_iota(jnp.int32, (CAND_ROWS, tr), 0).astype(F32)
    for j in range(2 * ROUTER_HEADS):
        s_ref[j] = _dot_nt(keys_ref[j // 2, j % 2], q_ref[:, j * half:(j + 1) * half])

    def refill_scores():
        w_ref[...] = s_ref[...]

    _topk_rank([(w_ref.at[j], rk_ref.at[j], sv_ref.at[j], rows_ref) for j in range(2 * ROUTER_HEADS)],
               refill_scores)
    tail = CAND_ROWS - CAND_WIDE

    def refill_candidates():
        for g in range(ROUTER_HEADS):
            sv1, sv2, cand = sv_ref.at[2 * g], sv_ref.at[2 * g + 1], cand_ref.at[g]
            cand[0:PEER_TOPK, :] = sv1[0:1, :] + sv2[...]
            for k1 in range(1, CAND_WIDE):
                r0 = CAND_WIDE * (k1 + 1)
                cand[r0:r0 + CAND_WIDE, :] = sv1[k1:k1 + 1, :] + sv2[0:CAND_WIDE, :]
            cand[tail:CAND_ROWS, :] = sv1[CAND_WIDE:PEER_TOPK, :] + sv2[0:1, :]

    _topk_rank([(cand_ref.at[g], rc_ref.at[g], svc_ref.at[g], crow_ref) for g in range(ROUTER_HEADS)],
               refill_candidates)
    for g in range(ROUTER_HEADS):
        sv1, sv2 = sv_ref[2 * g], sv_ref[2 * g + 1]
        g1 = jnp.exp(sv1 - sv1[0:1, :])
        g2 = jnp.exp(sv2 - sv2[0:1, :])
        sel = jnp.where(rc_ref[g] < float(PEER_TOPK), 1.0, 0.0)
        r1 = rk_ref[2 * g]
        blk = sel[0:PEER_TOPK, :]
        z = jnp.sum(blk * g2, axis=0, keepdims=True)
        lf = jnp.where(r1 == 0.0, jnp.sum(blk, axis=0, keepdims=True), 0.0)
        for k1 in range(1, CAND_WIDE):
            r0 = CAND_WIDE * (k1 + 1)
            blk = sel[r0:r0 + CAND_WIDE, :]
            z = z + g1[k1:k1 + 1, :] * jnp.sum(blk * g2[0:CAND_WIDE, :], axis=0, keepdims=True)
            lf = lf + jnp.where(r1 == float(k1), jnp.sum(blk, axis=0, keepdims=True), 0.0)
        blk = sel[tail:CAND_ROWS, :]
        z = z + jnp.sum(blk * g1[CAND_WIDE:PEER_TOPK, :], axis=0, keepdims=True)
        for r in range(CAND_WIDE):
            lf = lf + jnp.where(r1 == float(CAND_WIDE + r), blk[r:r + 1, :], 0.0)
        lf_ref[g] = _bf16_twice(lf)
        e1_ref[g] = _bf16_twice(jnp.exp(s_ref[2 * g] - sv1[0:1, :]) / z)
        r2_ref[g] = pltpu.bitcast(rk_ref[2 * g + 1].astype(BF16), jnp.uint32)
        e2_ref[g] = pltpu.bitcast(jnp.exp(s_ref[2 * g + 1] - sv2[0:1, :]).astype(BF16), jnp.uint32)


def _router(qp, keys, tr):
    n = qp.shape[0]
    tab = pl.BlockSpec((ROUTER_HEADS, N_KEYS, tr), lambda t, h: (h, 0, t))
    ptab = pl.BlockSpec((ROUTER_HEADS, N_KEYS // 2, tr), lambda t, h: (h, 0, t))
    twice_tab = jax.ShapeDtypeStruct((PEER_HEADS, N_KEYS, n), jnp.uint32)
    packed_tab = jax.ShapeDtypeStruct((PEER_HEADS, N_KEYS // 2, n), jnp.uint32)
    vm = lambda lead, rows: pltpu.VMEM((lead, rows, tr), F32)
    return pl.pallas_call(
        _router_kernel,
        grid=(n // tr, PEER_HEADS // ROUTER_HEADS),
        in_specs=[pl.BlockSpec((tr, ROUTER_HEADS * PEER_DK), lambda t, h: (t, h)),
                  pl.BlockSpec((ROUTER_HEADS, 2, N_KEYS, PEER_DK // 2), lambda t, h: (h, 0, 0, 0))],
        out_specs=[tab, tab, ptab, ptab],
        out_shape=[twice_tab, twice_tab, packed_tab, packed_tab],
        scratch_shapes=[vm(2 * ROUTER_HEADS, N_KEYS), vm(2 * ROUTER_HEADS, N_KEYS), vm(2 * ROUTER_HEADS, N_KEYS),
                        vm(2 * ROUTER_HEADS, PEER_TOPK), vm(ROUTER_HEADS, CAND_ROWS), vm(ROUTER_HEADS, CAND_ROWS),
                        vm(ROUTER_HEADS, PEER_TOPK), pltpu.VMEM((N_KEYS, tr), F32),
                        pltpu.VMEM((CAND_ROWS, tr), F32)],
        compiler_params=_params("parallel", "arbitrary"),
    )(qp, keys)


def _gelu(x):
    return 0.5 * x * (1.0 + lax.erf(x * (2.0 ** -0.5)))


PACK = 16


TILE_COPIES = 4


def _experts_kernel(x_ref, h_ref, lf_ref, e1_ref, r2_ref, e2_ref, *refs, et):
    u_refs, v_refs = refs[:TILE_COPIES], refs[TILE_COPIES:2 * TILE_COPIES]
    gfin_ref, o_ref, s_ref, a_ref, lrow_ref, erow_ref = refs[2 * TILE_COPIES:]
    e = pl.program_id(1)
    tp = x_ref.shape[0]

    @pl.when(e == 0)
    def _():
        o_ref[...] = h_ref[...]

    u_tile = jnp.concatenate([r[...] for r in u_refs], axis=0)
    s_ref[...] = _dot_nt(u_tile, x_ref[...])
    zero = jnp.zeros((PACK, LANES), BF16)
    n_sub = et // N_KEYS
    i0 = pl.multiple_of(e * n_sub, n_sub)
    for h in range(PEER_HEADS):
        lrow_ref[h] = lf_ref[h, pl.ds(i0, n_sub), :]
        erow_ref[h] = e1_ref[h, pl.ds(i0, n_sub), :]

    def row(ref, h, sub, ls):
        words = jnp.broadcast_to(ref[h, sub:sub + 1, ls], (SUBLANES, LANES))
        return pltpu.bitcast(words, BF16)

    for tb in range(tp // LANES):
        ls = slice(tb * LANES, (tb + 1) * LANES)
        for sub in range(n_sub):
            lrows = [row(lrow_ref, h, sub, ls) for h in range(PEER_HEADS)]
            erows = [row(erow_ref, h, sub, ls) for h in range(PEER_HEADS)]
            for c in range(N_KEYS // PACK):
                js = slice(c * SUBLANES, (c + 1) * SUBLANES)
                rs = slice(sub * N_KEYS + c * PACK, sub * N_KEYS + (c + 1) * PACK)
                w = None
                for h in range(PEER_HEADS):
                    r2 = pltpu.bitcast(r2_ref[h, js, ls], BF16)
                    e2 = pltpu.bitcast(e2_ref[h, js, ls], BF16)
                    t = jnp.where(r2 < lrows[h], e2, zero) * erows[h]
                    w = t if w is None else w + t
                a_ref[rs, ls] = w * _gelu(s_ref[rs, ls]).astype(BF16)
    v_tile = jnp.concatenate([r[...] for r in v_refs], axis=0)
    o_ref[...] += _dot_tn(a_ref[...], v_tile)

    @pl.when(e == pl.num_programs(1) - 1)
    def _():
        o_ref[...] = _rms(o_ref[...], gfin_ref[...])


def _experts(xn, h2, tabs, u_bf, v_bf, g_final, tp, et):
    n, d = xn.shape
    assert et % (SUBLANES * N_KEYS) == 0, "a step's first-half keys must be whole sublane tiles of the gate tables"
    tab = pl.BlockSpec((PEER_HEADS, N_KEYS, tp), lambda t, e: (0, 0, t))
    ptab = pl.BlockSpec((PEER_HEADS, N_KEYS // 2, tp), lambda t, e: (0, 0, t))
    row = pl.BlockSpec((tp, d), lambda t, e: (t, 0))
    slab = et // TILE_COPIES
    wts = [pl.BlockSpec((slab, d), functools.partial(lambda t, e, m: (e * TILE_COPIES + m, 0), m=m))
           for m in range(TILE_COPIES)]
    return pl.pallas_call(
        functools.partial(_experts_kernel, et=et),
        grid=(n // tp, N_EXPERTS // et),
        in_specs=[row, row, tab, tab, ptab, ptab] + wts + wts + [_resident((1, d))],
        out_specs=row,
        out_shape=jax.ShapeDtypeStruct((n, d), F32),
        scratch_shapes=[pltpu.VMEM((et, tp), F32), pltpu.VMEM((et, tp), BF16),
                        pltpu.VMEM((PEER_HEADS, et // N_KEYS, tp), jnp.uint32),
                        pltpu.VMEM((PEER_HEADS, et // N_KEYS, tp), jnp.uint32)],
        compiler_params=_params("parallel", "arbitrary"),
    )(xn, h2, *tabs, *([u_bf] * TILE_COPIES), *([v_bf] * TILE_COPIES), g_final.reshape(1, d))


def _layer(x, pos_offset, conv_hist, k_hist, v_hist, mk, mv, wts, g_final, tb, tp):
    (g_mix, w_in, dw_w, dw_b, ln_g, ln_b, sinks, w_out, g_cross, w_cq, w_co, g_ffn, wq, keys, u, v) = wts
    b, t, d = x.shape
    glu, q, k, vv = _in_proj(x, g_mix, w_in, pos_offset, tb)
    if conv_hist is None:
        conv_hist = jnp.zeros((b, HIST, D_CONV), F32)
    conv_out = _conv(glu, conv_hist, dw_w, dw_b, ln_g, ln_b, tb)
    if k_hist is None:
        attn = _attn_prompt(q, k, vv, sinks)
        new_k, new_v = k[:, -WINDOW:], vv[:, -WINDOW:]
    else:
        attn = _attn_cached(q, k_hist, v_hist, k, vv, sinks)
        new_k = jnp.concatenate([k_hist, k], axis=1)[:, -WINDOW:]
        new_v = jnp.concatenate([v_hist, vv], axis=1)[:, -WINDOW:]
    h2, xn, qp = _mid(x, conv_out, attn, mk, mv, w_out, g_cross, w_cq, w_co, g_ffn, wq, tb)
    n = b * t
    tabs = _router(qp.reshape(n, -1), keys, min(256, n))
    y = _experts(xn.reshape(n, d), h2.reshape(n, d), tabs, u, v, g_final, tp, 1024)
    new_conv = glu[:, -HIST:]
    return (y.reshape(b, t, d), new_conv,
            new_k.reshape(b, WINDOW, N_KV_HEADS, HEAD_DIM), new_v.reshape(b, WINDOW, N_KV_HEADS, HEAD_DIM))


def kernel(x_prompt, x_sample, mem_prompt, cache_conv, cache_swa_k, cache_swa_v, cache_mem_k, cache_mem_v, norm_mix_g, w_in, conv_dw_w, conv_dw_b, conv_ln_g, conv_ln_b, swa_sinks, w_out, norm_cross_g, norm_mem_g, w_cq, w_ck, w_cv, w_co, norm_ffn_g, peer_wq, peer_subkeys, peer_u, peer_v, norm_final_g):
    depth = w_in.shape[0]
    assert depth == 1, "the final rmsnorm is fused into the last layer's expert kernel"
    bs = x_sample.shape[0]
    l = 0
    bf = lambda a: a.astype(BF16)
    wts = (norm_mix_g[l], bf(w_in[l]), conv_dw_w[l], conv_dw_b[l], conv_ln_g[l], conv_ln_b[l], swa_sinks[l],
           bf(w_out[l]), norm_cross_g[l], bf(w_cq[l]), bf(w_co[l]), norm_ffn_g[l], bf(peer_wq[l]),
           peer_subkeys[l], bf(peer_u[l]), bf(peer_v[l]))
    mk, mv = _mem_kv(mem_prompt, norm_mem_g[l], bf(w_ck[l]), bf(w_cv[l]))
    yp, conv_p, k_p, v_p = _layer(x_prompt, 0, None, None, None, mk, mv, wts, norm_final_g,
                                  tb=256, tp=512)
    ys, conv_s, k_s, v_s = _layer(
        x_sample, PAST_LEN, cache_conv[l], cache_swa_k[l].reshape(bs, WINDOW, KV_DIM),
        cache_swa_v[l].reshape(bs, WINDOW, KV_DIM), cache_mem_k[l].reshape(bs, N_MEM, D_MEM),
        cache_mem_v[l].reshape(bs, N_MEM, D_MEM), wts, norm_final_g, tb=x_sample.shape[1], tp=256)
    bp = x_prompt.shape[0]
    mem_shape = (1, bp, N_MEM, MEM_HEADS, MEM_HEAD_DIM)
    return (yp, ys, conv_p[None], k_p[None], v_p[None], mk.reshape(mem_shape), mv.reshape(mem_shape),
            conv_s[None], k_s[None], v_s[None])
```
